```python
import math
import jax, jax.numpy as jnp
from jax import lax
import numpy as np


D_MODEL = 1024
BATCH = 8
SEQ = 2048
DEPTH = 4
DEC_BATCH = 128
DEC_SEQ = 4
PAST_LEN = 16384
PAGE_SIZE = 128

D_SSD = D_MODEL
SSD_HEAD_DIM = 64
SSD_HEADS = D_SSD // SSD_HEAD_DIM
SSD_GROUPS = 2
HEADS_PER_GROUP = SSD_HEADS // SSD_GROUPS
D_STATE = 128
CONV_WIDTH = 4
CONV_DIM = D_SSD + 2 * SSD_GROUPS * D_STATE
SSD_CHUNK = 128
POOL_WIDTH = D_MODEL
POOL_WINDOWS = (2, 4, 8, 16)
POOL_GROUPS = len(POOL_WINDOWS)
POOL_GROUP_DIM = POOL_WIDTH // POOL_GROUPS
POOL_BUF = max(POOL_WINDOWS) - 1
D_MIX = D_SSD + POOL_WIDTH
D_IN_PROJ = D_SSD + CONV_DIM + SSD_HEADS + POOL_WIDTH
D_FF = 2816
PLE_DIM = 256
EPS = 1e-6

kernel_name = "hymba_ssd_multiscale_pool_macaron_decoder_step"


def rms_norm(x, w):
    xf = x.astype(jnp.float32)
    y = xf * lax.rsqrt(jnp.mean(xf * xf, axis=-1, keepdims=True) + EPS)
    return (y * w.astype(jnp.float32)).astype(x.dtype)


def swiglu(u, wg, wu, wd):
    return (jax.nn.silu(u @ wg) * (u @ wu)) @ wd


def causal_dwconv(x_ext, w, b):
    c = x_ext.shape[-1]
    y = lax.conv_general_dilated(x_ext, w[:, None, :].astype(x_ext.dtype), window_strides=(1,), padding='VALID',
                                 dimension_numbers=('NWC', 'WIO', 'NWC'), feature_group_count=c)
    return y + b.astype(x_ext.dtype)


def ssd_scan(xh, dt, a, bm, cm, h0):
    b, l, g, r, p = xh.shape
    n = bm.shape[-1]
    q = math.gcd(l, SSD_CHUNK)
    c = l // q
    f32 = jnp.float32
    dt = dt.astype(f32)
    da = (dt * a).reshape(b, c, q, g, r)
    xdt = (xh.astype(f32) * dt[..., None]).reshape(b, c, q, g, r, p)
    bc = bm.astype(f32).reshape(b, c, q, g, n)
    cc = cm.astype(f32).reshape(b, c, q, g, n)
    a_cum = jnp.cumsum(da, axis=2)
    causal = jnp.tril(jnp.ones((q, q), bool))[None, None, :, :, None, None]
    seg = a_cum[:, :, :, None] - a_cum[:, :, None, :]
    decay_in = jnp.where(causal, jnp.exp(jnp.where(causal, seg, 0.0)), 0.0)
    scores = jnp.einsum('bcqgn,bcsgn->bcqsg', cc, bc)
    w_in_chunk = scores[..., None] * decay_in
    y_diag = jnp.einsum('bcqsgr,bcsgrp->bcqgrp', w_in_chunk, xdt)
    decay_out = jnp.exp(a_cum[:, :, -1:] - a_cum)
    chunk_states = jnp.einsum('bcsgn,bcsgrp->bcgrpn', bc, xdt * decay_out[..., None])
    chunk_decay = jnp.exp(a_cum[:, :, -1])

    def step(h_prev, inp):
        cs, cd = inp
        return h_prev * cd[..., None, None] + cs, h_prev

    h_final, h_prev = lax.scan(step, h0.astype(f32),
                               (jnp.moveaxis(chunk_states, 1, 0), jnp.moveaxis(chunk_decay, 1, 0)))
    h_prev = jnp.moveaxis(h_prev, 0, 1)
    y_off = jnp.einsum('bcqgn,bcgrpn->bcqgrp', cc, h_prev) * jnp.exp(a_cum)[..., None]
    y = (y_diag + y_off).reshape(b, l, g, r, p)
    return y, h_final


def multiscale_pool(x_ext, n_valid, pool_w, pool_scale):
    b, le, ch = x_ext.shape
    l = le - POOL_BUF
    xf = x_ext.astype(jnp.float32)
    cs = jnp.concatenate([jnp.zeros((b, 1, ch), jnp.float32), jnp.cumsum(xf, axis=1)], axis=1)
    t = jnp.arange(l)
    outs = []
    for gi, w in enumerate(POOL_WINDOWS):
        lo, hi = gi * POOL_GROUP_DIM, (gi + 1) * POOL_GROUP_DIM
        s = cs[:, POOL_BUF + 1:, lo:hi] - cs[:, POOL_BUF + 1 - w:POOL_BUF + 1 - w + l, lo:hi]
        cnt = jnp.minimum(n_valid + t + 1, w).astype(jnp.float32)
        outs.append(s / cnt[None, :, None] - xf[:, POOL_BUF:, lo:hi])
    d = jnp.stack(outs, axis=2)
    y = jnp.einsum('blgc,gcd->blgd', d, pool_w.astype(jnp.float32)).reshape(b, l, ch)
    return (y * pool_scale.astype(jnp.float32)).astype(x_ext.dtype)


def mixer(u, conv_buf, h0, pool_buf, n_valid_pool, w_in, conv_w, conv_b, dt_bias, a_log, d_skip,
          ssd_norm_w, pool_w, pool_scale, w_out):
    b, l, _ = u.shape
    proj = u @ w_in
    o1, o2, o3 = D_SSD, D_SSD + CONV_DIM, D_SSD + CONV_DIM + SSD_HEADS
    z, xbc, dt, xp = proj[..., :o1], proj[..., o1:o2], proj[..., o2:o3], proj[..., o3:]
    xbc_ext = jnp.concatenate([conv_buf.astype(xbc.dtype), xbc], axis=1)
    new_conv = xbc_ext[:, -(CONV_WIDTH - 1):]
    xbc_c = jax.nn.silu(causal_dwconv(xbc_ext, conv_w, conv_b))
    xs = xbc_c[..., :D_SSD].reshape(b, l, SSD_GROUPS, HEADS_PER_GROUP, SSD_HEAD_DIM)
    gn = SSD_GROUPS * D_STATE
    bm = xbc_c[..., D_SSD:D_SSD + gn].reshape(b, l, SSD_GROUPS, D_STATE)
    cm = xbc_c[..., D_SSD + gn:].reshape(b, l, SSD_GROUPS, D_STATE)
    dtp = jax.nn.softplus((dt + dt_bias).astype(jnp.float32)).reshape(b, l, SSD_GROUPS, HEADS_PER_GROUP)
    a = -jnp.exp(a_log.astype(jnp.float32)).reshape(SSD_GROUPS, HEADS_PER_GROUP)
    h0g = h0.reshape(b, SSD_GROUPS, HEADS_PER_GROUP, SSD_HEAD_DIM, D_STATE)
    y, h = ssd_scan(xs, dtp, a, bm, cm, h0g)
    y = y + d_skip.astype(jnp.float32).reshape(SSD_GROUPS, HEADS_PER_GROUP)[..., None] * xs.astype(jnp.float32)
    y = y.reshape(b, l, D_SSD).astype(u.dtype)
    y = rms_norm(y * jax.nn.silu(z), ssd_norm_w)
    pool_ext = jnp.concatenate([pool_buf.astype(xp.dtype), xp], axis=1)
    new_pool = pool_ext[:, -POOL_BUF:]
    yp = multiscale_pool(pool_ext, n_valid_pool, pool_w, pool_scale)
    out = jnp.concatenate([y, yp], axis=-1) @ w_out
    new_h = h.reshape(b, SSD_HEADS, SSD_HEAD_DIM, D_STATE).astype(h0.dtype)
    return out, new_conv, new_h, new_pool


def trunk(x, p, ssm0, conv0, pool0, n_valid_pool, w_in, conv_w, conv_b, dt_bias, a_log, d_skip, ssd_norm_w,
          pool_w, pool_scale, w_out, norm_ffn1, ffn1_gate, ffn1_up, ffn1_down, norm_mix, norm_ffn2,
          ffn2_gate, ffn2_up, ffn2_down, norm_ple, ple_gate, ple_proj, final_norm):
    ssm_out, conv_out, pool_out = [], [], []
    for i in range(DEPTH):
        x = x + 0.5 * swiglu(rms_norm(x, norm_ffn1[i]), ffn1_gate[i], ffn1_up[i], ffn1_down[i])
        m, nc, nh, npool = mixer(rms_norm(x, norm_mix[i]), conv0[i], ssm0[i], pool0[i], n_valid_pool,
                                 w_in[i], conv_w[i], conv_b[i], dt_bias[i], a_log[i], d_skip[i],
                                 ssd_norm_w[i], pool_w[i], pool_scale[i], w_out[i])
        x = x + m
        x = x + 0.5 * swiglu(rms_norm(x, norm_ffn2[i]), ffn2_gate[i], ffn2_up[i], ffn2_down[i])
        gate = jax.nn.sigmoid(rms_norm(x, norm_ple[i]) @ ple_gate[i])
        x = x + gate * (p[i].astype(x.dtype) @ ple_proj[i])
        ssm_out.append(nh)
        conv_out.append(nc)
        pool_out.append(npool)
    y = rms_norm(x, final_norm)
    return y, jnp.stack(ssm_out), jnp.stack(conv_out), jnp.stack(pool_out)


def setup_inputs(seed: int = 0) -> dict:
    key = jax.random.key(seed)
    ks = jax.random.split(key, 32)
    f32 = jnp.float32

    def nrm(k, shape, fan_in):
        return jax.random.normal(k, shape, f32) * (fan_in ** -0.5)

    def gain(k, shape):
        return 1.0 + 0.02 * jax.random.normal(k, shape, f32)

    dt0 = jnp.exp(jax.random.uniform(ks[10], (DEPTH, SSD_HEADS), f32) * (math.log(0.1) - math.log(0.001))
                  + math.log(0.001))
    return {
        "x_prompt": jax.random.normal(ks[0], (BATCH, SEQ, D_MODEL), f32),
        "x_sample": jax.random.normal(ks[1], (DEC_BATCH, DEC_SEQ, D_MODEL), f32),
        "p_prompt": jax.random.normal(ks[2], (DEPTH, BATCH, SEQ, PLE_DIM), f32),
        "p_sample": jax.random.normal(ks[3], (DEPTH, DEC_BATCH, DEC_SEQ, PLE_DIM), f32),
        "state_ssm": 0.1 * jax.random.normal(ks[4], (DEPTH, DEC_BATCH, SSD_HEADS, SSD_HEAD_DIM, D_STATE), f32),
        "state_conv": jax.random.normal(ks[5], (DEPTH, DEC_BATCH, CONV_WIDTH - 1, CONV_DIM), f32),
        "state_pool": jax.random.normal(ks[6], (DEPTH, DEC_BATCH, POOL_BUF, POOL_WIDTH), f32),
        "w_in": nrm(ks[7], (DEPTH, D_MODEL, D_IN_PROJ), D_MODEL),
        "conv_w": nrm(ks[8], (DEPTH, CONV_WIDTH, CONV_DIM), CONV_WIDTH),
        "conv_b": 0.01 * jax.random.normal(ks[9], (DEPTH, CONV_DIM), f32),
        "dt_bias": dt0 + jnp.log(-jnp.expm1(-dt0)),
        "a_log": jnp.log(jax.random.uniform(ks[11], (DEPTH, SSD_HEADS), f32, 1.0, 16.0)),
        "d_skip": gain(ks[12], (DEPTH, SSD_HEADS)),
        "ssd_norm_w": gain(ks[13], (DEPTH, D_SSD)),
        "pool_w": nrm(ks[14], (DEPTH, POOL_GROUPS, POOL_GROUP_DIM, POOL_GROUP_DIM), POOL_GROUP_DIM),
        "pool_scale": gain(ks[15], (DEPTH, POOL_WIDTH)),
        "w_out": nrm(ks[16], (DEPTH, D_MIX, D_MODEL), D_MIX),
        "norm_ffn1": gain(ks[17], (DEPTH, D_MODEL)),
        "ffn1_gate": nrm(ks[18], (DEPTH, D_MODEL, D_FF), D_MODEL),
        "ffn1_up": nrm(ks[19], (DEPTH, D_MODEL, D_FF), D_MODEL),
        "ffn1_down": nrm(ks[20], (DEPTH, D_FF, D_MODEL), D_FF),
        "norm_mix": gain(ks[21], (DEPTH, D_MODEL)),
        "norm_ffn2": gain(ks[22], (DEPTH, D_MODEL)),
        "ffn2_gate": nrm(ks[23], (DEPTH, D_MODEL, D_FF), D_MODEL),
        "ffn2_up": nrm(ks[24], (DEPTH, D_MODEL, D_FF), D_MODEL),
        "ffn2_down": nrm(ks[25], (DEPTH, D_FF, D_MODEL), D_FF),
        "norm_ple": gain(ks[26], (DEPTH, D_MODEL)),
        "ple_gate": nrm(ks[27], (DEPTH, D_MODEL, D_MODEL), D_MODEL),
        "ple_proj": nrm(ks[28], (DEPTH, PLE_DIM, D_MODEL), PLE_DIM),
        "final_norm": gain(ks[29], (D_MODEL,)),
    }


def reference(x_prompt, x_sample, p_prompt, p_sample, state_ssm, state_conv, state_pool, w_in, conv_w, conv_b,
              dt_bias, a_log, d_skip, ssd_norm_w, pool_w, pool_scale, w_out, norm_ffn1, ffn1_gate, ffn1_up,
              ffn1_down, norm_mix, norm_ffn2, ffn2_gate, ffn2_up, ffn2_down, norm_ple, ple_gate, ple_proj,
              final_norm):
    dt_ = x_prompt.dtype
    b = x_prompt.shape[0]
    ssm0 = jnp.zeros((DEPTH, b, SSD_HEADS, SSD_HEAD_DIM, D_STATE), dt_)
    conv0 = jnp.zeros((DEPTH, b, CONV_WIDTH - 1, CONV_DIM), dt_)
    pool0 = jnp.zeros((DEPTH, b, POOL_BUF, POOL_WIDTH), dt_)
    weights = (w_in, conv_w, conv_b, dt_bias, a_log, d_skip, ssd_norm_w, pool_w, pool_scale, w_out,
               norm_ffn1, ffn1_gate, ffn1_up, ffn1_down, norm_mix, norm_ffn2, ffn2_gate, ffn2_up, ffn2_down,
               norm_ple, ple_gate, ple_proj, final_norm)
    y_prompt, ssm_prompt, conv_prompt, pool_prompt = trunk(x_prompt, p_prompt, ssm0, conv0, pool0, 0, *weights)
    y_sample, ssm_sample, conv_sample, pool_sample = trunk(x_sample, p_sample, state_ssm, state_conv, state_pool,
                                                           min(PAST_LEN, POOL_BUF), *weights)
    return (y_prompt, y_sample, ssm_prompt, conv_prompt, pool_prompt, ssm_sample, conv_sample, pool_sample)
```

```python
import functools
import math

import jax
import jax.numpy as jnp
from jax import lax
from jax.experimental import pallas as pl
from jax.experimental.pallas import tpu as pltpu

F32 = jnp.float32
BF16 = jnp.bfloat16

EPS = 1e-6
D_MODEL = 1024
D_SSD = 1024
HEAD_DIM = 64
N_HEADS = 16
N_GROUPS = 2
HEADS_PER_GROUP = 8
D_STATE = 128
CONV_W = 4
CONV_DIM = D_SSD + 2 * N_GROUPS * D_STATE
CHUNK = 128
POOL_WINDOWS = (2, 4, 8, 16)
POOL_GD = 256
POOL_BUF = 15
POOL_W = 1024
D_FF = 2816
PLE_DIM = 256
LANES = 128
O_Z, O_XBC, O_XP, O_DT, O_END = 0, 1024, 2560, 3584, 3712
VMEM_LIMIT = 56 * 1024 * 1024


def _dot(a, b):
    return jnp.dot(a, b, preferred_element_type=F32)


def _dot_t0(a, b):
    return lax.dot_general(a, b, (((0,), (0,)), ((), ())), preferred_element_type=F32)


def _dot_t1(a, b):
    return lax.dot_general(a, b, (((1,), (1,)), ((), ())), preferred_element_type=F32)


def _rms(x, w):
    return x * lax.rsqrt(jnp.mean(x * x, axis=-1, keepdims=True) + EPS) * w


def _silu(x):
    return x * jax.nn.sigmoid(x)


def _softplus(x):
    return jnp.maximum(x, 0.0) + jnp.log1p(jnp.exp(-jnp.abs(x)))


def _split3(v):
    hi = v.astype(BF16)
    r1 = v - hi.astype(F32)
    mid = r1.astype(BF16)
    lo = (r1 - mid.astype(F32)).astype(BF16)
    return jnp.concatenate([hi, mid, lo], axis=1)


def _expand_heads(v, e3):
    return _dot(_split3(v), e3)


def _ffn_kernel(*refs, n_ff, ple, final):
    if ple:
        (x_ref, nw_ref, wg_ref, wu_ref, wd_ref, p_ref, npw_ref, pg_ref, pp_ref, fn_ref,
         o_ref, u_sc, acc_sc) = refs
    else:
        x_ref, nw_ref, wg_ref, wu_ref, wd_ref, o_ref, u_sc, acc_sc = refs
    j = pl.program_id(1)

    @pl.when(j == 0)
    def _():
        u_sc[...] = _rms(x_ref[...], nw_ref[...]).astype(BF16)

    u = u_sc[...]
    g = _dot(u, wg_ref[...])
    up = _dot(u, wu_ref[...])
    h = (_silu(g) * up).astype(BF16)
    d = _dot(h, wd_ref[...])

    @pl.when(j == 0)
    def _():
        acc_sc[...] = d

    @pl.when(j > 0)
    def _():
        acc_sc[...] += d

    @pl.when(j == n_ff - 1)
    def _():
        x1 = x_ref[...] + 0.5 * acc_sc[...]
        if ple:
            un = _rms(x1, npw_ref[...]).astype(BF16)
            gate = jax.nn.sigmoid(_dot(un, pg_ref[...]))
            x1 = x1 + gate * _dot(p_ref[...], pp_ref[...])
            if final:
                x1 = _rms(x1, fn_ref[...])
        o_ref[...] = x1


def _ffn_call(x, layer, nw, wg, wu, wd, tm, tf, ple=None, final=False):
    t, d = x.shape
    ff = wg.shape[-1]
    n_ff = ff // tf
    grid = (t // tm, n_ff)
    in_specs = [
        pl.BlockSpec((tm, d), lambda i, j: (i, 0)),
        pl.BlockSpec((None, 1, d), lambda i, j: (layer, 0, 0)),
        pl.BlockSpec((None, d, tf), lambda i, j: (layer, 0, j)),
        pl.BlockSpec((None, d, tf), lambda i, j: (layer, 0, j)),
        pl.BlockSpec((None, tf, d), lambda i, j: (layer, j, 0)),
    ]
    args = [x, nw, wg, wu, wd]
    if ple is not None:
        p, npw, pg, pp, fn = ple
        pd = p.shape[-1]
        in_specs += [
            pl.BlockSpec((None, tm, pd), lambda i, j: (layer, i, 0)),
            pl.BlockSpec((None, 1, d), lambda i, j: (layer, 0, 0)),
            pl.BlockSpec((None, d, d), lambda i, j: (layer, 0, 0)),
            pl.BlockSpec((None, pd, d), lambda i, j: (layer, 0, 0)),
            pl.BlockSpec((1, d), lambda i, j: (0, 0)),
        ]
        args += [p, npw, pg, pp, fn]
    return pl.pallas_call(
        functools.partial(_ffn_kernel, n_ff=n_ff, ple=ple is not None, final=final),
        grid=grid,
        in_specs=in_specs,
        out_specs=pl.BlockSpec((tm, d), lambda i, j: (i, 0)),
        out_shape=jax.ShapeDtypeStruct((t, d), F32),
        scratch_shapes=[pltpu.VMEM((tm, d), BF16), pltpu.VMEM((tm, d), F32)],
        compiler_params=pltpu.CompilerParams(
            dimension_semantics=("parallel", "arbitrary"), vmem_limit_bytes=VMEM_LIMIT),
        name="ffn_ple" if ple is not None else "ffn",
    )(*args)


def _ssd_chunk(c, dt_sc, xbc_sc, y_sc, h_sc, a_row, dsk, e3, tril, causal, lo_mask):
    r0 = pl.multiple_of(c * CHUNK, CHUNK)
    rows = pl.ds(r0, CHUNK)
    dtc = dt_sc[rows, :]
    acum = _dot(tril, _split3(dtc * a_row))
    acum = acum[:, 0:LANES] + acum[:, LANES:2 * LANES] + acum[:, 2 * LANES:3 * LANES]
    a_t = acum.T[0:N_HEADS, :]
    dt_t = dtc.T[0:N_HEADS, :]
    w_t = dt_t * jnp.exp(a_t[:, CHUNK - 1:CHUNK] - a_t)
    cd = jnp.exp(_expand_heads(acum[CHUNK - 8:CHUNK, :], e3)[7:8, :])
    for g in range(N_GROUPS):
        b_g = xbc_sc[rows, D_SSD + g * D_STATE:D_SSD + (g + 1) * D_STATE]
        c_g = xbc_sc[rows, D_SSD + (N_GROUPS + g) * D_STATE:D_SSD + (N_GROUPS + g + 1) * D_STATE]
        bt_g = b_g.T
        s_g = _dot(c_g.astype(BF16), bt_g.astype(BF16))
        for k in range(HEADS_PER_GROUP // 2):
            lb = g * (HEADS_PER_GROUP // 2) + k
            cols = slice(lb * LANES, (lb + 1) * LANES)
            lhs, btw = [], []
            for r in (2 * lb, 2 * lb + 1):
                lq = acum[:, r:r + 1]
                dec = jnp.where(causal, jnp.exp(lq - a_t[r:r + 1, :]), 0.0)
                lhs.append((s_g * dec * dt_t[r:r + 1, :]).astype(BF16))
                lhs.append((c_g * jnp.exp(lq)).astype(BF16))
                btw.append((bt_g * w_t[r:r + 1, :]).astype(BF16))
            xs = xbc_sc[rows, cols]
            hp = h_sc[:, cols]
            x_lo = jnp.where(lo_mask, xs, 0.0).astype(BF16)
            x_hi = jnp.where(lo_mask, 0.0, xs).astype(BF16)
            h_lo = jnp.where(lo_mask, hp, 0.0).astype(BF16)
            h_hi = jnp.where(lo_mask, 0.0, hp).astype(BF16)
            y = _dot(jnp.concatenate(lhs, axis=1), jnp.concatenate([x_lo, h_lo, x_hi, h_hi], axis=0))
            y_sc[rows, cols] = y + dsk[:, cols] * xs
            upd = _dot(jnp.concatenate(btw, axis=1), jnp.concatenate([x_lo, x_hi], axis=0))
            h_sc[:, cols] = hp * cd[:, cols] + upd


def _pool_group_sums(ext, gi, w):
    s = ext[:, gi * POOL_GD:(gi + 1) * POOL_GD]
    sh = 1
    while sh < w:
        s = s + pltpu.roll(s, sh, 0)
        sh *= 2
    return s[16:, :]


def _mixp_kernel(x_ref, nw_ref, win_ref, cw_ref, cb_ref, dtb_ref, alog_ref, dsk_ref, snw_ref, pw_ref,
                 ps_ref, wout_ref, e3_ref,
                 xo_ref, ssm_ref, conv_ref, pool_ref,
                 h_sc, cext_sc, pext_sc, xbc_sc, dt_sc, y_sc, *, L, n_tiles):
    i = pl.program_id(1)

    @pl.when(i == 0)
    def _():
        h_sc[...] = jnp.zeros_like(h_sc)
        cext_sc[0:8, :] = jnp.zeros((8, CONV_DIM), F32)
        pext_sc[0:16, :] = jnp.zeros((16, POOL_W), F32)

    x = x_ref[...]
    u = _rms(x, nw_ref[...]).astype(BF16)
    z = _dot(u, win_ref[:, O_Z:O_XBC])
    xbc = _dot(u, win_ref[:, O_XBC:O_XP])
    xp = _dot(u, win_ref[:, O_XP:O_DT])
    dtr = _dot(u, win_ref[:, O_DT:O_END])

    cext_sc[8:8 + L, :] = xbc
    cw = cw_ref[...]
    conv = cb_ref[...] + cw[3:4, :] * xbc
    for k in range(CONV_W - 1):
        conv = conv + cw[k:k + 1, :] * cext_sc[5 + k:5 + k + L, :]
    xbc_sc[...] = _silu(conv)

    dt_sc[...] = _softplus(dtr + dtb_ref[...])
    a_row = -jnp.exp(alog_ref[...])

    ri = lax.broadcasted_iota(jnp.int32, (CHUNK, CHUNK), 0)
    ci = lax.broadcasted_iota(jnp.int32, (CHUNK, CHUNK), 1)
    causal = ri >= ci
    tril = causal.astype(BF16)
    lo_mask = ci < HEAD_DIM
    dsk = dsk_ref[...]
    e3 = e3_ref[...]

    def chunk_body(c, carry):
        _ssd_chunk(c, dt_sc, xbc_sc, y_sc, h_sc, a_row, dsk, e3, tril, causal, lo_mask)
        return carry

    lax.fori_loop(0, L // CHUNK, chunk_body, 0)

    yn = _rms(y_sc[...] * _silu(z), snw_ref[...]).astype(BF16)

    pext_sc[16:16 + L, :] = xp
    ext = pext_sc[...]
    t_abs = i * L + lax.broadcasted_iota(jnp.int32, (L, 1), 0)
    yps = []
    for gi, w in enumerate(POOL_WINDOWS):
        cnt = jnp.minimum(t_abs + 1, w).astype(F32)
        d = _pool_group_sums(ext, gi, w) / cnt - xp[:, gi * POOL_GD:(gi + 1) * POOL_GD]
        yps.append(_dot(d.astype(BF16), pw_ref[gi]))
    yp = (jnp.concatenate(yps, axis=1) * ps_ref[...]).astype(BF16)

    out = _dot(jnp.concatenate([yn, yp], axis=1), wout_ref[...])
    xo_ref[...] = x + out

    @pl.when(i == n_tiles - 1)
    def _():
        ssm_ref[...] = h_sc[...].T
        conv_ref[...] = cext_sc[L + 5:L + 8, :]
        pool_ref[...] = pext_sc[L + 1:L + 16, :]

    cext_sc[0:8, :] = cext_sc[L:L + 8, :]
    pext_sc[0:16, :] = pext_sc[L:L + 16, :]


def _const_spec(shape, layer=None):
    if layer is None:
        return pl.BlockSpec(shape, lambda b, i: (0,) * len(shape))
    return pl.BlockSpec((None,) + shape, lambda b, i: (layer,) + (0,) * len(shape))


def _mixp_call(x, layer, w, batch, seq, L):
    n_tiles = seq // L
    d = x.shape[-1]
    in_specs = [
        pl.BlockSpec((L, d), lambda b, i: (b * n_tiles + i, 0)),
        _const_spec((1, d), layer),
        _const_spec((d, O_END), layer),
        _const_spec((CONV_W, CONV_DIM), layer),
        _const_spec((1, CONV_DIM), layer),
        _const_spec((1, LANES), layer),
        _const_spec((1, LANES), layer),
        _const_spec((1, D_SSD), layer),
        _const_spec((1, D_SSD), layer),
        _const_spec((len(POOL_WINDOWS), POOL_GD, POOL_GD), layer),
        _const_spec((1, POOL_W), layer),
        _const_spec((D_SSD + POOL_W, d), layer),
        _const_spec((3 * LANES, D_SSD)),
    ]
    out_specs = [
        pl.BlockSpec((L, d), lambda b, i: (b * n_tiles + i, 0)),
        pl.BlockSpec((None, D_SSD, D_STATE), lambda b, i: (b, 0, 0)),
        pl.BlockSpec((None, CONV_W - 1, CONV_DIM), lambda b, i: (b, 0, 0)),
        pl.BlockSpec((None, POOL_BUF, POOL_W), lambda b, i: (b, 0, 0)),
    ]
    out_shape = [
        jax.ShapeDtypeStruct((batch * seq, d), F32),
        jax.ShapeDtypeStruct((batch, D_SSD, D_STATE), F32),
        jax.ShapeDtypeStruct((batch, CONV_W - 1, CONV_DIM), F32),
        jax.ShapeDtypeStruct((batch, POOL_BUF, POOL_W), F32),
    ]
    scratch = [
        pltpu.VMEM((D_STATE, D_SSD), F32),
        pltpu.VMEM((8 + L, CONV_DIM), F32),
        pltpu.VMEM((16 + L, POOL_W), F32),
        pltpu.VMEM((L, CONV_DIM), F32),
        pltpu.VMEM((L, LANES), F32),
        pltpu.VMEM((L, D_SSD), F32),
    ]
    return pl.pallas_call(
        functools.partial(_mixp_kernel, L=L, n_tiles=n_tiles),
        grid=(batch, n_tiles),
        in_specs=in_specs,
        out_specs=out_specs,
        out_shape=out_shape,
        scratch_shapes=scratch,
        compiler_params=pltpu.CompilerParams(
            dimension_semantics=("parallel", "arbitrary"), vmem_limit_bytes=VMEM_LIMIT),
        name="mixer_prompt",
    )(x, w["norm_mix"], w["w_in"], w["conv_w"], w["conv_b"], w["dt_bias"], w["a_log"], w["d_skip"],
      w["ssd_norm_w"], w["pool_w"], w["pool_scale"], w["w_out"], w["e3"])


def _sa_kernel(x_ref, sconv_ref, spool_ref, nw_ref, win_ref, cw_ref, cb_ref, dtb_ref, alog_ref, dsk_ref,
               pw_ref, ps_ref, e3_ref,
               z_ref, yp_ref, ypart_ref, ea_ref, xw_ref, cd3_ref, c_ref, b_ref, conv_ref, pool_ref,
               *, nb, nt):
    u = _rms(x_ref[...], nw_ref[...]).astype(BF16)
    z_ref[...] = _dot(u, win_ref[:, O_Z:O_XBC])
    xbc = _dot(u, win_ref[:, O_XBC:O_XP])
    xp = _dot(u, win_ref[:, O_XP:O_DT])
    dtr = _dot(u, win_ref[:, O_DT:O_END])
    e3 = e3_ref[...]

    def tile(v, t):
        return v[t * nb:(t + 1) * nb, :]

    cext = [sconv_ref[:, k * CONV_DIM:(k + 1) * CONV_DIM] for k in range(CONV_W - 1)]
    cext += [tile(xbc, t) for t in range(nt)]
    cw = cw_ref[...]
    xs, bm, cm = [], [], []
    for t in range(nt):
        acc = cb_ref[...]
        for k in range(CONV_W):
            acc = acc + cw[k:k + 1, :] * cext[t + k]
        v = _silu(acc)
        xs.append(v[:, 0:D_SSD])
        bm.append(v[:, D_SSD:D_SSD + N_GROUPS * D_STATE])
        cm.append(v[:, D_SSD + N_GROUPS * D_STATE:])
    conv_ref[...] = jnp.concatenate(cext[nt:], axis=1)

    dt = _softplus(dtr + dtb_ref[...])
    a_row = -jnp.exp(alog_ref[...])
    dts = [tile(dt, t) for t in range(nt)]
    acum = []
    for t in range(nt):
        da = dts[t] * a_row
        acum.append(da if t == 0 else acum[-1] + da)
    a_last = acum[-1]

    lane = lax.broadcasted_iota(jnp.int32, (nb, LANES), 1)
    g0_heads = lane < HEADS_PER_GROUP
    dsk = dsk_ref[...]
    for t in range(nt):
        y = dsk * xs[t]
        for s in range(t + 1):
            sc = []
            for g in range(N_GROUPS):
                cg = cm[t][:, g * D_STATE:(g + 1) * D_STATE]
                bg = bm[s][:, g * D_STATE:(g + 1) * D_STATE]
                sc.append(jnp.sum(cg * bg, axis=1, keepdims=True))
            wts = jnp.where(g0_heads, sc[0], sc[1]) * jnp.exp(acum[t] - acum[s]) * dts[s]
            y = y + _expand_heads(wts, e3) * xs[s]
        ypart_ref[t * nb:(t + 1) * nb, :] = y
        ea_ref[t * nb:(t + 1) * nb, :] = _expand_heads(jnp.exp(acum[t]), e3)
        xw_ref[t * nb:(t + 1) * nb, :] = _expand_heads(dts[t] * jnp.exp(a_last - acum[t]), e3) * xs[t]
        c_ref[t * nb:(t + 1) * nb, :] = cm[t]
        b_ref[t * nb:(t + 1) * nb, :] = bm[t]
    cd = jnp.exp(_expand_heads(a_last, e3))
    hi = cd.astype(BF16).astype(F32)
    mid = (cd - hi).astype(BF16).astype(F32)
    lo = (cd - hi - mid).astype(BF16).astype(F32)
    cd3_ref[0 * nb:1 * nb, :] = hi
    cd3_ref[1 * nb:2 * nb, :] = mid
    cd3_ref[2 * nb:3 * nb, :] = lo

    pext = [spool_ref[:, k * POOL_W:(k + 1) * POOL_W] for k in range(POOL_BUF)]
    pext += [tile(xp, t) for t in range(nt)]
    for t in range(nt):
        yps = []
        for gi, w in enumerate(POOL_WINDOWS):
            cols = slice(gi * POOL_GD, (gi + 1) * POOL_GD)
            s = pext[POOL_BUF + t][:, cols]
            for k in range(1, w):
                s = s + pext[POOL_BUF + t - k][:, cols]
            d = s / float(w) - pext[POOL_BUF + t][:, cols]
            yps.append(_dot(d.astype(BF16), pw_ref[gi]))
        yp_ref[t * nb:(t + 1) * nb, :] = (jnp.concatenate(yps, axis=1) * ps_ref[...]).astype(BF16)
    pool_ref[...] = jnp.concatenate(pext[nt:], axis=1)


def _sa_call(x, sconv, spool, layer, w, nb, nt):
    rows = nt * nb
    d = x.shape[-1]

    def cs(shape, lyr=None):
        if lyr is None:
            return pl.BlockSpec(shape, lambda i: (0,) * len(shape))
        return pl.BlockSpec((None,) + shape, lambda i: (lyr,) + (0,) * len(shape))

    in_specs = [
        cs((rows, d)),
        cs((nb, (CONV_W - 1) * CONV_DIM), layer),
        cs((nb, POOL_BUF * POOL_W), layer),
        cs((1, d), layer),
        cs((d, O_END), layer),
        cs((CONV_W, CONV_DIM), layer),
        cs((1, CONV_DIM), layer),
        cs((1, LANES), layer),
        cs((1, LANES), layer),
        cs((1, D_SSD), layer),
        cs((len(POOL_WINDOWS), POOL_GD, POOL_GD), layer),
        cs((1, POOL_W), layer),
        cs((3 * LANES, D_SSD)),
    ]
    gn = N_GROUPS * D_STATE
    outs = [
        ((rows, D_SSD), F32),
        ((rows, POOL_W), BF16),
        ((rows, D_SSD), F32),
        ((rows, D_SSD), F32),
        ((rows, D_SSD), F32),
        ((3 * nb, D_SSD), F32),
        ((rows, gn), F32),
        ((rows, gn), F32),
        ((nb, (CONV_W - 1) * CONV_DIM), F32),
        ((nb, POOL_BUF * POOL_W), F32),
    ]
    return pl.pallas_call(
        functools.partial(_sa_kernel, nb=nb, nt=nt),
        grid=(1,),
        in_specs=in_specs,
        out_specs=[cs(s) for s, _ in outs],
        out_shape=[jax.ShapeDtypeStruct(s, dt) for s, dt in outs],
        compiler_params=pltpu.CompilerParams(vmem_limit_bytes=VMEM_LIMIT),
        name="mixer_decode_tokens",
    )(x, sconv, spool, w["norm_mix"], w["w_in"], w["conv_w"], w["conv_b"], w["dt_bias"], w["a_log"],
      w["d_skip"], w["pool_w"], w["pool_scale"], w["e3"])


def _sb_kernel(h_ref, c_ref, b_ref, xw_ref, cd3_ref, hn_ref, ch_ref, *, sblk):
    half = D_SSD // N_GROUPS
    ones = jnp.ones((8, D_STATE), BF16)
    for j in range(sblk):
        h0 = h_ref[j]
        hb = h0.astype(BF16)
        cj = c_ref[j].astype(BF16)
        bj = b_ref[j].astype(BF16)
        xw = xw_ref[j].astype(BF16)
        dmat = _dot_t0(cd3_ref[j].astype(BF16), ones)
        chs, upds = [], []
        for g in range(N_GROUPS):
            rs = slice(g * half, (g + 1) * half)
            ns = slice(g * D_STATE, (g + 1) * D_STATE)
            chs.append(_dot_t1(cj[:, ns], hb[rs, :]))
            upds.append(_dot_t0(xw[:, rs], bj[:, ns]))
        ch_ref[j] = jnp.concatenate(chs, axis=1)
        hn_ref[j] = h0 * dmat + jnp.concatenate(upds, axis=0)


def _sb_call(h0, c8, b8, xw8, cd38, layer, nb, sblk):
    gn = N_GROUPS * D_STATE
    return pl.pallas_call(
        functools.partial(_sb_kernel, sblk=sblk),
        grid=(nb // sblk,),
        in_specs=[
            pl.BlockSpec((None, sblk, D_SSD, D_STATE), lambda i: (layer, i, 0, 0)),
            pl.BlockSpec((sblk, 8, gn), lambda i: (i, 0, 0)),
            pl.BlockSpec((sblk, 8, gn), lambda i: (i, 0, 0)),
            pl.BlockSpec((sblk, 8, D_SSD), lambda i: (i, 0, 0)),
            pl.BlockSpec((sblk, 8, D_SSD), lambda i: (i, 0, 0)),
        ],
        out_specs=[
            pl.BlockSpec((sblk, D_SSD, D_STATE), lambda i: (i, 0, 0)),
            pl.BlockSpec((sblk, 8, D_SSD), lambda i: (i, 0, 0)),
        ],
        out_shape=[
            jax.ShapeDtypeStruct((nb, D_SSD, D_STATE), F32),
            jax.ShapeDtypeStruct((nb, 8, D_SSD), F32),
        ],
        compiler_params=pltpu.CompilerParams(
            dimension_semantics=("parallel",), vmem_limit_bytes=VMEM_LIMIT),
        name="mixer_decode_state",
    )(h0, c8, b8, xw8, cd38)


def _sc_kernel(x_ref, z_ref, ypart_ref, ea_ref, ch_ref, yp_ref, snw_ref, wout_ref, o_ref):
    y = ypart_ref[...] + ea_ref[...] * ch_ref[...]
    yn = _rms(y * _silu(z_ref[...]), snw_ref[...]).astype(BF16)
    out = _dot(jnp.concatenate([yn, yp_ref[...]], axis=1), wout_ref[...])
    o_ref[...] = x_ref[...] + out


def _sc_call(x, z, ypart, ea, ch, yp, layer, w):
    rows, d = x.shape

    def cs(shape, lyr=None):
        if lyr is None:
            return pl.BlockSpec(shape, lambda i: (0,) * len(shape))
        return pl.BlockSpec((None,) + shape, lambda i: (lyr,) + (0,) * len(shape))

    return pl.pallas_call(
        _sc_kernel,
        grid=(1,),
        in_specs=[cs((rows, d)), cs((rows, D_SSD)), cs((rows, D_SSD)), cs((rows, D_SSD)), cs((rows, D_SSD)),
                  cs((rows, POOL_W)), cs((1, D_SSD), layer), cs((D_SSD + POOL_W, d), layer)],
        out_specs=cs((rows, d)),
        out_shape=jax.ShapeDtypeStruct((rows, d), F32),
        compiler_params=pltpu.CompilerParams(vmem_limit_bytes=VMEM_LIMIT),
        name="mixer_decode_out",
    )(x, z, ypart, ea, ch, yp, w["ssd_norm_w"], w["w_out"])


def _to_seq_major8(v, nb, nt):
    v = v.reshape(-1, nb, v.shape[-1])
    v = jnp.pad(v, ((0, 8 - v.shape[0]), (0, 0), (0, 0)))
    return jnp.transpose(v, (1, 0, 2))


def _mixer_decode(x, layer, state_ssm, sconv, spool, w, nb, nt, sblk):
    z, yp, ypart, ea, xw, cd3, cm, bm, nconv, npool = _sa_call(x, sconv, spool, layer, w, nb, nt)
    hn, ch8 = _sb_call(state_ssm, _to_seq_major8(cm, nb, nt), _to_seq_major8(bm, nb, nt),
                       _to_seq_major8(xw, nb, nt), _to_seq_major8(cd3, nb, 3), layer, nb, sblk)
    ch = jnp.transpose(ch8[:, :nt, :], (1, 0, 2)).reshape(nt * nb, D_SSD)
    x2 = _sc_call(x, z, ypart, ea, ch, yp, layer, w)
    return x2, hn, nconv, npool


def _prep_weights(w_in, conv_w, conv_b, dt_bias, a_log, d_skip, ssd_norm_w, pool_w, pool_scale, w_out,
                  norm_mix):
    depth = w_in.shape[0]
    o1, o2, o3 = D_SSD, D_SSD + CONV_DIM, D_SSD + CONV_DIM + N_HEADS
    w_dt = jnp.pad(w_in[:, :, o2:o3], ((0, 0), (0, 0), (0, LANES - N_HEADS)))
    w_in_r = jnp.concatenate([w_in[:, :, :o2], w_in[:, :, o3:], w_dt], axis=-1).astype(BF16)
    pad_h = ((0, 0), (0, LANES - N_HEADS))
    head_of_lane = jnp.arange(D_SSD) // HEAD_DIM
    e1 = (jnp.arange(LANES)[:, None] == head_of_lane[None, :]).astype(BF16)
    return {
        "norm_mix": norm_mix.reshape(depth, 1, -1),
        "w_in": w_in_r,
        "conv_w": conv_w,
        "conv_b": conv_b.reshape(depth, 1, -1),
        "dt_bias": jnp.pad(dt_bias, pad_h).reshape(depth, 1, LANES),
        "a_log": jnp.pad(a_log, pad_h).reshape(depth, 1, LANES),
        "d_skip": jnp.repeat(d_skip, HEAD_DIM, axis=-1).reshape(depth, 1, D_SSD),
        "ssd_norm_w": ssd_norm_w.reshape(depth, 1, -1),
        "pool_w": pool_w.astype(BF16),
        "pool_scale": pool_scale.reshape(depth, 1, -1),
        "w_out": w_out.astype(BF16),
        "e3": jnp.concatenate([e1, e1, e1], axis=0),
    }


def _ssm_to_out(h, batch):
    return h.reshape(batch, N_HEADS, HEAD_DIM, D_STATE)


def kernel(x_prompt, x_sample, p_prompt, p_sample, state_ssm, state_conv, state_pool, w_in, conv_w, conv_b,
           dt_bias, a_log, d_skip, ssd_norm_w, pool_w, pool_scale, w_out, norm_ffn1, ffn1_gate, ffn1_up,
           ffn1_down, norm_mix, norm_ffn2, ffn2_gate, ffn2_up, ffn2_down, norm_ple, ple_gate, ple_proj,
           final_norm, *, tm=512, tf=1408, L=256, sblk=8):
    depth = w_in.shape[0]
    batch, seq, d = x_prompt.shape
    nb, nt, _ = x_sample.shape
    mw = _prep_weights(w_in, conv_w, conv_b, dt_bias, a_log, d_skip, ssd_norm_w, pool_w, pool_scale, w_out,
                       norm_mix)
    f1 = (norm_ffn1.reshape(depth, 1, d), ffn1_gate.astype(BF16), ffn1_up.astype(BF16), ffn1_down.astype(BF16))
    f2 = (norm_ffn2.reshape(depth, 1, d), ffn2_gate.astype(BF16), ffn2_up.astype(BF16), ffn2_down.astype(BF16))
    npw = norm_ple.reshape(depth, 1, d)
    pg = ple_gate.astype(BF16)
    pp = ple_proj.astype(BF16)
    fn = final_norm.reshape(1, d)

    xp_ = x_prompt.reshape(batch * seq, d)
    pp_prompt = p_prompt.reshape(depth, batch * seq, -1).astype(BF16)
    xs_ = jnp.transpose(x_sample, (1, 0, 2)).reshape(nt * nb, d)
    pp_sample = jnp.transpose(p_sample, (0, 2, 1, 3)).reshape(depth, nt * nb, -1).astype(BF16)
    h_all = state_ssm.reshape(depth, nb, D_SSD, D_STATE)
    sconv_all = state_conv.reshape(depth, nb, -1)
    spool_all = state_pool.reshape(depth, nb, -1)
    tm_s = min(tm, nt * nb)

    ssm_p, conv_p, pool_p, ssm_s, conv_s, pool_s = [], [], [], [], [], []
    for i in range(depth):
        last = i == depth - 1
        xp_ = _ffn_call(xp_, i, *f1, tm, tf)
        xs_ = _ffn_call(xs_, i, *f1, tm_s, tf)
        xp_, hp, cp, plp = _mixp_call(xp_, i, mw, batch, seq, L)
        xs_, hs, cs_, pls = _mixer_decode(xs_, i, h_all, sconv_all, spool_all, mw, nb, nt, sblk)
        xp_ = _ffn_call(xp_, i, *f2, tm, tf, ple=(pp_prompt, npw, pg, pp, fn), final=last)
        xs_ = _ffn_call(xs_, i, *f2, tm_s, tf, ple=(pp_sample, npw, pg, pp, fn), final=last)
        ssm_p.append(_ssm_to_out(hp, batch))
        conv_p.append(cp)
        pool_p.append(plp)
        ssm_s.append(_ssm_to_out(hs, nb))
        conv_s.append(cs_.reshape(nb, CONV_W - 1, CONV_DIM))
        pool_s.append(pls.reshape(nb, POOL_BUF, POOL_W))
    y_prompt = xp_.reshape(batch, seq, d)
    y_sample = jnp.transpose(xs_.reshape(nt, nb, d), (1, 0, 2))
    return (y_prompt, y_sample, jnp.stack(ssm_p), jnp.stack(conv_p), jnp.stack(pool_p),
            jnp.stack(ssm_s), jnp.stack(conv_s), jnp.stack(pool_s))
```

```python
import functools
import math

import jax
import jax.numpy as jnp
from jax import lax
from jax.experimental import pallas as pl
from jax.experimental.pallas import tpu as pltpu

F32 = jnp.float32
BF16 = jnp.bfloat16

EPS = 1e-6
D_MODEL = 1024
D_SSD = 1024
HEAD_DIM = 64
N_HEADS = 16
N_GROUPS = 2
HEADS_PER_GROUP = 8
D_STATE = 128
CONV_W = 4
CONV_DIM = D_SSD + 2 * N_GROUPS * D_STATE
CHUNK = 128
POOL_WINDOWS = (2, 4, 8, 16)
POOL_GD = 256
POOL_BUF = 15
POOL_W = 1024
D_FF = 2816
PLE_DIM = 256
LANES = 128
O_Z, O_XBC, O_XP, O_DT, O_END = 0, 1024, 2560, 3584, 3712
VMEM_LIMIT = 56 * 1024 * 1024


def _dot(a, b):
    return jnp.dot(a, b, preferred_element_type=F32)


def _dot_t0(a, b):
    return lax.dot_general(a, b, (((0,), (0,)), ((), ())), preferred_element_type=F32)


def _dot_t1(a, b):
    return lax.dot_general(a, b, (((1,), (1,)), ((), ())), preferred_element_type=F32)


def _rms(x, w):
    return x * lax.rsqrt(jnp.mean(x * x, axis=-1, keepdims=True) + EPS) * w


def _silu(x):
    return x * jax.nn.sigmoid(x)


def _softplus(x):
    return jnp.maximum(x, 0.0) + jnp.log1p(jnp.exp(-jnp.abs(x)))


def _split3(v):
    hi = v.astype(BF16)
    r1 = v - hi.astype(F32)
    mid = r1.astype(BF16)
    lo = (r1 - mid.astype(F32)).astype(BF16)
    return jnp.concatenate([hi, mid, lo], axis=1)


def _expand_heads(v, e3):
    return _dot(_split3(v), e3)


FF_CHUNK = 512


def _ffn_kernel(*refs, ple, final):
    if ple:
        x_ref, nw_ref, wg_ref, wu_ref, wd_ref, p_ref, npw_ref, pg_ref, pp_ref, fn_ref, o_ref = refs
    else:
        x_ref, nw_ref, wg_ref, wu_ref, wd_ref, o_ref = refs
    x = x_ref[...]
    u = _rms(x, nw_ref[...]).astype(BF16)
    ff = wg_ref.shape[-1]
    acc = None
    for c0 in range(0, ff, FF_CHUNK):
        c1 = min(c0 + FF_CHUNK, ff)
        g = _dot(u, wg_ref[:, c0:c1])
        up = _dot(u, wu_ref[:, c0:c1])
        d = _dot((_silu(g) * up).astype(BF16), wd_ref[c0:c1, :])
        acc = d if acc is None else acc + d
    x1 = x + 0.5 * acc
    if ple:
        un = _rms(x1, npw_ref[...]).astype(BF16)
        gate = jax.nn.sigmoid(_dot(un, pg_ref[...]))
        x1 = x1 + gate * _dot(p_ref[...], pp_ref[...])
        if final:
            x1 = _rms(x1, fn_ref[...])
    o_ref[...] = x1


def _resident(shape, layer=None):
    n = len(shape)
    if layer is None:
        return pl.BlockSpec(shape, lambda *_: (0,) * n, pipeline_mode=pl.Buffered(1))
    return pl.BlockSpec((None,) + shape, lambda *_: (layer,) + (0,) * n, pipeline_mode=pl.Buffered(1))


def _ffn_call(x, layer, nw, wg, wu, wd, tm, ple=None, final=False):
    t, d = x.shape
    ff = wg.shape[-1]
    in_specs = [
        pl.BlockSpec((tm, d), lambda i: (i, 0)),
        _resident((1, d), layer),
        _resident((d, ff), layer),
        _resident((d, ff), layer),
        _resident((ff, d), layer),
    ]
    args = [x, nw, wg, wu, wd]
    if ple is not None:
        p, npw, pg, pp, fn = ple
        pd = p.shape[-1]
        in_specs += [
            pl.BlockSpec((None, tm, pd), lambda i: (layer, i, 0)),
            _resident((1, d), layer),
            _resident((d, d), layer),
            _resident((pd, d), layer),
            _resident((1, d)),
        ]
        args += [p, npw, pg, pp, fn]
    return pl.pallas_call(
        functools.partial(_ffn_kernel, ple=ple is not None, final=final),
        grid=(t // tm,),
        in_specs=in_specs,
        out_specs=pl.BlockSpec((tm, d), lambda i: (i, 0)),
        out_shape=jax.ShapeDtypeStruct((t, d), F32),
        compiler_params=pltpu.CompilerParams(
            dimension_semantics=("parallel",), vmem_limit_bytes=VMEM_LIMIT),
        name="ffn_ple" if ple is not None else "ffn",
    )(*args)


def _ssd_chunk(c, dt_sc, xbc_sc, y_sc, h_sc, a_row, dsk, e3, tril, causal, lo_mask):
    r0 = pl.multiple_of(c * CHUNK, CHUNK)
    rows = pl.ds(r0, CHUNK)
    dtc = dt_sc[rows, :]
    acum = _dot(tril, _split3(dtc * a_row))
    acum = acum[:, 0:LANES] + acum[:, LANES:2 * LANES] + acum[:, 2 * LANES:3 * LANES]
    a_t = acum.T[0:N_HEADS, :]
    dt_t = dtc.T[0:N_HEADS, :]
    w_t = dt_t * jnp.exp(a_t[:, CHUNK - 1:CHUNK] - a_t)
    cd = jnp.exp(_expand_heads(acum[CHUNK - 8:CHUNK, :], e3)[7:8, :])
    for g in range(N_GROUPS):
        b_g = xbc_sc[rows, D_SSD + g * D_STATE:D_SSD + (g + 1) * D_STATE]
        c_g = xbc_sc[rows, D_SSD + (N_GROUPS + g) * D_STATE:D_SSD + (N_GROUPS + g + 1) * D_STATE]
        bt_g = b_g.T
        s_g = _dot(c_g.astype(BF16), bt_g.astype(BF16))
        for k in range(HEADS_PER_GROUP // 2):
            lb = g * (HEADS_PER_GROUP // 2) + k
            cols = slice(lb * LANES, (lb + 1) * LANES)
            lhs, btw = [], []
            for r in (2 * lb, 2 * lb + 1):
                lq = acum[:, r:r + 1]
                dec = jnp.where(causal, jnp.exp(lq - a_t[r:r + 1, :]), 0.0)
                lhs.append((s_g * dec * dt_t[r:r + 1, :]).astype(BF16))
                lhs.append((c_g * jnp.exp(lq)).astype(BF16))
                btw.append((bt_g * w_t[r:r + 1, :]).astype(BF16))
            xs = xbc_sc[rows, cols]
            hp = h_sc[:, cols]
            x_lo = jnp.where(lo_mask, xs, 0.0).astype(BF16)
            x_hi = jnp.where(lo_mask, 0.0, xs).astype(BF16)
            h_lo = jnp.where(lo_mask, hp, 0.0).astype(BF16)
            h_hi = jnp.where(lo_mask, 0.0, hp).astype(BF16)
            y = _dot(jnp.concatenate(lhs, axis=1), jnp.concatenate([x_lo, h_lo, x_hi, h_hi], axis=0))
            y_sc[rows, cols] = y + dsk[:, cols] * xs
            upd = _dot(jnp.concatenate(btw, axis=1), jnp.concatenate([x_lo, x_hi], axis=0))
            h_sc[:, cols] = hp * cd[:, cols] + upd


def _pool_group_sums(ext, gi, w):
    s = ext[:, gi * POOL_GD:(gi + 1) * POOL_GD]
    sh = 1
    while sh < w:
        s = s + pltpu.roll(s, sh, 0)
        sh *= 2
    return s[16:, :]


def _mixp_kernel(x_ref, nw_ref, win_ref, cw_ref, cb_ref, dtb_ref, alog_ref, dsk_ref, snw_ref, pw_ref,
                 ps_ref, wout_ref, e3_ref,
                 xo_ref, ssm_ref, conv_ref, pool_ref,
                 h_sc, cext_sc, pext_sc, xbc_sc, dt_sc, y_sc, *, L, n_tiles):
    i = pl.program_id(1)

    @pl.when(i == 0)
    def _():
        h_sc[...] = jnp.zeros_like(h_sc)
        cext_sc[0:8, :] = jnp.zeros((8, CONV_DIM), F32)
        pext_sc[0:16, :] = jnp.zeros((16, POOL_W), F32)

    x = x_ref[...]
    u = _rms(x, nw_ref[...]).astype(BF16)
    z = _dot(u, win_ref[:, O_Z:O_XBC])
    xbc = _dot(u, win_ref[:, O_XBC:O_XP])
    xp = _dot(u, win_ref[:, O_XP:O_DT])
    dtr = _dot(u, win_ref[:, O_DT:O_END])

    cext_sc[8:8 + L, :] = xbc
    cw = cw_ref[...]
    conv = cb_ref[...] + cw[3:4, :] * xbc
    for k in range(CONV_W - 1):
        conv = conv + cw[k:k + 1, :] * cext_sc[5 + k:5 + k + L, :]
    xbc_sc[...] = _silu(conv)

    dt_sc[...] = _softplus(dtr + dtb_ref[...])
    a_row = -jnp.exp(alog_ref[...])

    ri = lax.broadcasted_iota(jnp.int32, (CHUNK, CHUNK), 0)
    ci = lax.broadcasted_iota(jnp.int32, (CHUNK, CHUNK), 1)
    causal = ri >= ci
    tril = causal.astype(BF16)
    lo_mask = ci < HEAD_DIM
    dsk = dsk_ref[...]
    e3 = e3_ref[...]

    def chunk_body(c, carry):
        _ssd_chunk(c, dt_sc, xbc_sc, y_sc, h_sc, a_row, dsk, e3, tril, causal, lo_mask)
        return carry

    lax.fori_loop(0, L // CHUNK, chunk_body, 0)

    yn = _rms(y_sc[...] * _silu(z), snw_ref[...]).astype(BF16)

    pext_sc[16:16 + L, :] = xp
    ext = pext_sc[...]
    t_abs = i * L + lax.broadcasted_iota(jnp.int32, (L, 1), 0)
    yps = []
    for gi, w in enumerate(POOL_WINDOWS):
        cnt = jnp.minimum(t_abs + 1, w).astype(F32)
        d = _pool_group_sums(ext, gi, w) / cnt - xp[:, gi * POOL_GD:(gi + 1) * POOL_GD]
        yps.append(_dot(d.astype(BF16), pw_ref[gi]))
    yp = (jnp.concatenate(yps, axis=1) * ps_ref[...]).astype(BF16)

    out = _dot(jnp.concatenate([yn, yp], axis=1), wout_ref[...])
    xo_ref[...] = x + out

    @pl.when(i == n_tiles - 1)
    def _():
        ssm_ref[...] = h_sc[...].T
        conv_ref[...] = cext_sc[L + 5:L + 8, :]
        pool_ref[...] = pext_sc[L + 1:L + 16, :]

    cext_sc[0:8, :] = cext_sc[L:L + 8, :]
    pext_sc[0:16, :] = pext_sc[L:L + 16, :]


def _mixp_call(x, layer, w, depth, batch, seq, L, ssm_prev):
    n_tiles = seq // L
    t_all, d = x.shape
    in_specs = [
        pl.BlockSpec((L, d), lambda b, i: (b * n_tiles + i, 0)),
        _resident((1, d), layer),
        _resident((d, O_END), layer),
        _resident((CONV_W, CONV_DIM), layer),
        _resident((1, CONV_DIM), layer),
        _resident((1, LANES), layer),
        _resident((1, LANES), layer),
        _resident((1, D_SSD), layer),
        _resident((1, D_SSD), layer),
        _resident((len(POOL_WINDOWS), POOL_GD, POOL_GD), layer),
        _resident((1, POOL_W), layer),
        _resident((D_SSD + POOL_W, d), layer),
        _resident((3 * LANES, D_SSD)),
    ]
    args = [x, w["norm_mix"], w["w_in"], w["conv_w"], w["conv_b"], w["dt_bias"], w["a_log"], w["d_skip"],
            w["ssd_norm_w"], w["pool_w"], w["pool_scale"], w["w_out"], w["e3"]]
    n_in = len(args)
    aliases = {}
    if ssm_prev is not None:
        in_specs.append(pl.BlockSpec(memory_space=pl.ANY))
        aliases = {n_in: 1}
        args.append(ssm_prev)
    out_specs = [
        pl.BlockSpec((L, d), lambda b, i: (b * n_tiles + i, 0)),
        pl.BlockSpec((None, None, D_SSD, D_STATE), lambda b, i: (layer, b, 0, 0)),
        pl.BlockSpec((None, CONV_W - 1, CONV_DIM), lambda b, i: (b, 0, 0)),
        pl.BlockSpec((None, POOL_BUF, POOL_W), lambda b, i: (b, 0, 0)),
    ]
    out_shape = [
        jax.ShapeDtypeStruct((t_all, d), F32),
        jax.ShapeDtypeStruct((depth, batch, D_SSD, D_STATE), F32),
        jax.ShapeDtypeStruct((batch, CONV_W - 1, CONV_DIM), F32),
        jax.ShapeDtypeStruct((batch, POOL_BUF, POOL_W), F32),
    ]
    scratch = [
        pltpu.VMEM((D_STATE, D_SSD), F32),
        pltpu.VMEM((8 + L, CONV_DIM), F32),
        pltpu.VMEM((16 + L, POOL_W), F32),
        pltpu.VMEM((L, CONV_DIM), F32),
        pltpu.VMEM((L, LANES), F32),
        pltpu.VMEM((L, D_SSD), F32),
    ]

    def body(*refs):
        if ssm_prev is not None:
            refs = refs[:n_in] + refs[n_in + 1:]
        _mixp_kernel(*refs, L=L, n_tiles=n_tiles)

    return pl.pallas_call(
        body,
        grid=(batch, n_tiles),
        in_specs=in_specs,
        out_specs=out_specs,
        out_shape=out_shape,
        scratch_shapes=scratch,
        input_output_aliases=aliases,
        compiler_params=pltpu.CompilerParams(
            dimension_semantics=("parallel", "arbitrary"), vmem_limit_bytes=VMEM_LIMIT),
        name="mixer_prompt",
    )(*args)


def _store_seq8(ref, t, val):
    nb = val.shape[0]
    for k in range(ref.shape[0]):
        ref[k, pl.ds(t, nb, stride=8), :] = val[:, k * LANES:(k + 1) * LANES]


def _load_rows(ref, rows):
    return jnp.concatenate([ref[k, rows, :] for k in range(ref.shape[0])], axis=1)


def _sa_kernel(x_ref, sconv_ref, spool_ref, nw_ref, win_ref, cw_ref, cb_ref, dtb_ref, alog_ref, dsk_ref,
               pw_ref, ps_ref, e3_ref,
               z_ref, yp_ref, ypart_ref, ea_ref, xw_ref, cd3_ref, c_ref, b_ref, conv_ref, pool_ref,
               *, nb, nt):
    u = _rms(x_ref[...], nw_ref[...]).astype(BF16)
    z_ref[...] = _dot(u, win_ref[:, O_Z:O_XBC])
    xbc = _dot(u, win_ref[:, O_XBC:O_XP])
    xp = _dot(u, win_ref[:, O_XP:O_DT])
    dtr = _dot(u, win_ref[:, O_DT:O_END])
    e3 = e3_ref[...]

    def tile(v, t):
        return v[t * nb:(t + 1) * nb, :]

    cext = [sconv_ref[:, k * CONV_DIM:(k + 1) * CONV_DIM] for k in range(CONV_W - 1)]
    cext += [tile(xbc, t) for t in range(nt)]
    cw = cw_ref[...]
    xs, bm, cm = [], [], []
    for t in range(nt):
        acc = cb_ref[...]
        for k in range(CONV_W):
            acc = acc + cw[k:k + 1, :] * cext[t + k]
        v = _silu(acc)
        xs.append(v[:, 0:D_SSD])
        bm.append(v[:, D_SSD:D_SSD + N_GROUPS * D_STATE])
        cm.append(v[:, D_SSD + N_GROUPS * D_STATE:])
    conv_ref[...] = jnp.concatenate(cext[nt:], axis=1)

    dt = _softplus(dtr + dtb_ref[...])
    a_row = -jnp.exp(alog_ref[...])
    dts = [tile(dt, t) for t in range(nt)]
    acum = []
    for t in range(nt):
        da = dts[t] * a_row
        acum.append(da if t == 0 else acum[-1] + da)
    a_last = acum[-1]

    for r in (xw_ref, cd3_ref, c_ref, b_ref):
        r[...] = jnp.zeros(r.shape, F32)
    lane = lax.broadcasted_iota(jnp.int32, (nb, LANES), 1)
    g0_heads = lane < HEADS_PER_GROUP
    dsk = dsk_ref[...]
    for t in range(nt):
        y = dsk * xs[t]
        for s in range(t + 1):
            sc = []
            for g in range(N_GROUPS):
                cg = cm[t][:, g * D_STATE:(g + 1) * D_STATE]
                bg = bm[s][:, g * D_STATE:(g + 1) * D_STATE]
                sc.append(jnp.sum(cg * bg, axis=1, keepdims=True))
            wts = jnp.where(g0_heads, sc[0], sc[1]) * jnp.exp(acum[t] - acum[s]) * dts[s]
            y = y + _expand_heads(wts, e3) * xs[s]
        ypart_ref[t * nb:(t + 1) * nb, :] = y
        ea_ref[t * nb:(t + 1) * nb, :] = _expand_heads(jnp.exp(acum[t]), e3)
        _store_seq8(xw_ref, t, _expand_heads(dts[t] * jnp.exp(a_last - acum[t]), e3) * xs[t])
        _store_seq8(c_ref, t, cm[t])
        _store_seq8(b_ref, t, bm[t])
    cd = jnp.exp(_expand_heads(a_last, e3))
    hi = cd.astype(BF16).astype(F32)
    mid = (cd - hi).astype(BF16).astype(F32)
    lo = (cd - hi - mid).astype(BF16).astype(F32)
    _store_seq8(cd3_ref, 0, hi)
    _store_seq8(cd3_ref, 1, mid)
    _store_seq8(cd3_ref, 2, lo)

    pext = [spool_ref[:, k * POOL_W:(k + 1) * POOL_W] for k in range(POOL_BUF)]
    pext += [tile(xp, t) for t in range(nt)]
    for t in range(nt):
        yps = []
        for gi, w in enumerate(POOL_WINDOWS):
            cols = slice(gi * POOL_GD, (gi + 1) * POOL_GD)
            s = pext[POOL_BUF + t][:, cols]
            for k in range(1, w):
                s = s + pext[POOL_BUF + t - k][:, cols]
            d = s / float(w) - pext[POOL_BUF + t][:, cols]
            yps.append(_dot(d.astype(BF16), pw_ref[gi]))
        yp_ref[t * nb:(t + 1) * nb, :] = (jnp.concatenate(yps, axis=1) * ps_ref[...]).astype(BF16)
    pool_ref[...] = jnp.concatenate(pext[nt:], axis=1)


def _sa_call(x, sconv, spool, layer, w, depth, nb, nt, conv_prev, pool_prev):
    t_all, d = x.shape
    rows = nt * nb
    xblk = (t_all - rows) // rows
    in_specs = [
        pl.BlockSpec((rows, d), lambda i: (xblk, 0)),
        _resident((nb, (CONV_W - 1) * CONV_DIM), layer),
        _resident((nb, POOL_BUF * POOL_W), layer),
        _resident((1, d), layer),
        _resident((d, O_END), layer),
        _resident((CONV_W, CONV_DIM), layer),
        _resident((1, CONV_DIM), layer),
        _resident((1, LANES), layer),
        _resident((1, LANES), layer),
        _resident((1, D_SSD), layer),
        _resident((len(POOL_WINDOWS), POOL_GD, POOL_GD), layer),
        _resident((1, POOL_W), layer),
        _resident((3 * LANES, D_SSD)),
    ]
    args = [x, sconv, spool, w["norm_mix"], w["w_in"], w["conv_w"], w["conv_b"], w["dt_bias"], w["a_log"],
            w["d_skip"], w["pool_w"], w["pool_scale"], w["e3"]]
    n_in = len(args)
    aliases = {}
    if conv_prev is not None:
        in_specs += [pl.BlockSpec(memory_space=pl.ANY), pl.BlockSpec(memory_space=pl.ANY)]
        aliases = {n_in: 8, n_in + 1: 9}
        args += [conv_prev, pool_prev]
    gn = N_GROUPS * D_STATE
    outs = [
        ((rows, D_SSD), F32),
        ((rows, POOL_W), BF16),
        ((rows, D_SSD), F32),
        ((rows, D_SSD), F32),
        ((D_SSD // LANES, nb * 8, LANES), F32),
        ((D_SSD // LANES, nb * 8, LANES), F32),
        ((gn // LANES, nb * 8, LANES), F32),
        ((gn // LANES, nb * 8, LANES), F32),
    ]
    out_specs = [pl.BlockSpec(s, lambda i, n=len(s): (0,) * n) for s, _ in outs]
    out_shape = [jax.ShapeDtypeStruct(s, dt) for s, dt in outs]
    for width in ((CONV_W - 1) * CONV_DIM, POOL_BUF * POOL_W):
        out_specs.append(pl.BlockSpec((None, nb, width), lambda i: (layer, 0, 0)))
        out_shape.append(jax.ShapeDtypeStruct((depth, nb, width), F32))

    def body(*refs):
        if conv_prev is not None:
            refs = refs[:n_in] + refs[n_in + 2:]
        _sa_kernel(*refs, nb=nb, nt=nt)

    return pl.pallas_call(
        body,
        grid=(1,),
        in_specs=in_specs,
        out_specs=out_specs,
        out_shape=out_shape,
        input_output_aliases=aliases,
        compiler_params=pltpu.CompilerParams(vmem_limit_bytes=VMEM_LIMIT),
        name="mixer_decode_tokens",
    )(*args)


def _sb_kernel(h_ref, c_ref, b_ref, xw_ref, cd3_ref, hn_ref, ch_ref, *, sblk):
    half = D_SSD // N_GROUPS
    ones = jnp.ones((8, D_STATE), BF16)
    for j in range(sblk):
        rows = slice(j * 8, (j + 1) * 8)
        h0 = h_ref[j]
        hb = h0.astype(BF16)
        cj = _load_rows(c_ref, rows).astype(BF16)
        bj = _load_rows(b_ref, rows).astype(BF16)
        xw = _load_rows(xw_ref, rows).astype(BF16)
        dmat = _dot_t0(_load_rows(cd3_ref, rows).astype(BF16), ones)
        chs, upds = [], []
        for g in range(N_GROUPS):
            rs = slice(g * half, (g + 1) * half)
            ns = slice(g * D_STATE, (g + 1) * D_STATE)
            chs.append(_dot_t1(cj[:, ns], hb[rs, :]))
            upds.append(_dot_t0(xw[:, rs], bj[:, ns]))
        ch = jnp.concatenate(chs, axis=1)
        for k in range(ch_ref.shape[0]):
            ch_ref[k, rows, :] = ch[:, k * LANES:(k + 1) * LANES]
        hn_ref[j] = h0 * dmat + jnp.concatenate(upds, axis=0)


def _sb_call(h0, c8, b8, xw8, cd38, layer, depth, nb, sblk, ssm_prev):
    gn = N_GROUPS * D_STATE
    in_specs = [
        pl.BlockSpec((None, sblk, D_SSD, D_STATE), lambda i: (layer, i, 0, 0)),
        pl.BlockSpec((gn // LANES, sblk * 8, LANES), lambda i: (0, i, 0)),
        pl.BlockSpec((gn // LANES, sblk * 8, LANES), lambda i: (0, i, 0)),
        pl.BlockSpec((D_SSD // LANES, sblk * 8, LANES), lambda i: (0, i, 0)),
        pl.BlockSpec((D_SSD // LANES, sblk * 8, LANES), lambda i: (0, i, 0)),
    ]
    args = [h0, c8, b8, xw8, cd38]
    n_in = len(args)
    aliases = {}
    if ssm_prev is not None:
        in_specs.append(pl.BlockSpec(memory_space=pl.ANY))
        aliases = {n_in: 0}
        args.append(ssm_prev)

    def body(*refs):
        if ssm_prev is not None:
            refs = refs[:n_in] + refs[n_in + 1:]
        _sb_kernel(*refs, sblk=sblk)

    return pl.pallas_call(
        body,
        grid=(nb // sblk,),
        in_specs=in_specs,
        out_specs=[
            pl.BlockSpec((None, sblk, D_SSD, D_STATE), lambda i: (layer, i, 0, 0)),
            pl.BlockSpec((D_SSD // LANES, sblk * 8, LANES), lambda i: (0, i, 0)),
        ],
        out_shape=[
            jax.ShapeDtypeStruct((depth, nb, D_SSD, D_STATE), F32),
            jax.ShapeDtypeStruct((D_SSD // LANES, nb * 8, LANES), F32),
        ],
        input_output_aliases=aliases,
        compiler_params=pltpu.CompilerParams(
            dimension_semantics=("parallel",), vmem_limit_bytes=VMEM_LIMIT),
        name="mixer_decode_state",
    )(*args)


def _sc_kernel(x_ref, z_ref, ypart_ref, ea_ref, ch_ref, yp_ref, snw_ref, wout_ref, xprev_ref, o_ref, *, nb, nt):
    del xprev_ref
    ch = jnp.concatenate([_load_rows(ch_ref, pl.ds(t, nb, stride=8)) for t in range(nt)], axis=0)
    y = ypart_ref[...] + ea_ref[...] * ch
    yn = _rms(y * _silu(z_ref[...]), snw_ref[...]).astype(BF16)
    out = _dot(jnp.concatenate([yn, yp_ref[...]], axis=1), wout_ref[...])
    o_ref[...] = x_ref[...] + out


def _sc_call(x, z, ypart, ea, ch8, yp, layer, w, x_mixed, nb, nt):
    t_all, d = x.shape
    rows = nt * nb
    xblk = (t_all - rows) // rows

    def full(shape):
        return pl.BlockSpec(shape, lambda i: (0, 0))

    return pl.pallas_call(
        functools.partial(_sc_kernel, nb=nb, nt=nt),
        grid=(1,),
        in_specs=[pl.BlockSpec((rows, d), lambda i: (xblk, 0)), full((rows, D_SSD)), full((rows, D_SSD)),
                  full((rows, D_SSD)), pl.BlockSpec((D_SSD // LANES, nb * 8, LANES), lambda i: (0, 0, 0)),
                  full((rows, POOL_W)),
                  _resident((1, D_SSD), layer), _resident((D_SSD + POOL_W, d), layer),
                  pl.BlockSpec(memory_space=pl.ANY)],
        out_specs=pl.BlockSpec((rows, d), lambda i: (xblk, 0)),
        out_shape=jax.ShapeDtypeStruct((t_all, d), F32),
        input_output_aliases={8: 0},
        compiler_params=pltpu.CompilerParams(vmem_limit_bytes=VMEM_LIMIT),
        name="mixer_decode_out",
    )(x, z, ypart, ea, ch8, yp, w["ssd_norm_w"], w["w_out"], x_mixed)


def _prep_weights(w_in, conv_w, conv_b, dt_bias, a_log, d_skip, ssd_norm_w, pool_w, pool_scale, w_out,
                  norm_mix):
    depth = w_in.shape[0]
    o1, o2, o3 = D_SSD, D_SSD + CONV_DIM, D_SSD + CONV_DIM + N_HEADS
    w_dt = jnp.pad(w_in[:, :, o2:o3], ((0, 0), (0, 0), (0, LANES - N_HEADS)))
    w_in_r = jnp.concatenate([w_in[:, :, :o2], w_in[:, :, o3:], w_dt], axis=-1).astype(BF16)
    pad_h = ((0, 0), (0, LANES - N_HEADS))
    head_of_lane = jnp.arange(D_SSD) // HEAD_DIM
    e1 = (jnp.arange(LANES)[:, None] == head_of_lane[None, :]).astype(BF16)
    return {
        "norm_mix": norm_mix.reshape(depth, 1, -1),
        "w_in": w_in_r,
        "conv_w": conv_w,
        "conv_b": conv_b.reshape(depth, 1, -1),
        "dt_bias": jnp.pad(dt_bias, pad_h).reshape(depth, 1, LANES),
        "a_log": jnp.pad(a_log, pad_h).reshape(depth, 1, LANES),
        "d_skip": jnp.repeat(d_skip, HEAD_DIM, axis=-1).reshape(depth, 1, D_SSD),
        "ssd_norm_w": ssd_norm_w.reshape(depth, 1, -1),
        "pool_w": pool_w.astype(BF16),
        "pool_scale": pool_scale.reshape(depth, 1, -1),
        "w_out": w_out.astype(BF16),
        "e3": jnp.concatenate([e1, e1, e1], axis=0),
    }


def kernel(x_prompt, x_sample, p_prompt, p_sample, state_ssm, state_conv, state_pool, w_in, conv_w, conv_b,
           dt_bias, a_log, d_skip, ssd_norm_w, pool_w, pool_scale, w_out, norm_ffn1, ffn1_gate, ffn1_up,
           ffn1_down, norm_mix, norm_ffn2, ffn2_gate, ffn2_up, ffn2_down, norm_ple, ple_gate, ple_proj,
           final_norm, *, tm=768, L=256, sblk=8):
    depth = w_in.shape[0]
    batch, seq, d = x_prompt.shape
    nb, nt, _ = x_sample.shape
    n_prompt, n_dec = batch * seq, nt * nb
    mw = _prep_weights(w_in, conv_w, conv_b, dt_bias, a_log, d_skip, ssd_norm_w, pool_w, pool_scale, w_out,
                       norm_mix)
    f1 = (norm_ffn1.reshape(depth, 1, d), ffn1_gate.astype(BF16), ffn1_up.astype(BF16), ffn1_down.astype(BF16))
    f2 = (norm_ffn2.reshape(depth, 1, d), ffn2_gate.astype(BF16), ffn2_up.astype(BF16), ffn2_down.astype(BF16))
    ple_w = (norm_ple.reshape(depth, 1, d), ple_gate.astype(BF16), ple_proj.astype(BF16), final_norm.reshape(1, d))

    x = jnp.concatenate([x_prompt.reshape(n_prompt, d),
                         jnp.transpose(x_sample, (1, 0, 2)).reshape(n_dec, d)], axis=0)
    p = jnp.concatenate([p_prompt.reshape(depth, n_prompt, -1),
                         jnp.transpose(p_sample, (0, 2, 1, 3)).reshape(depth, n_dec, -1)], axis=1).astype(BF16)
    h_all = state_ssm.reshape(depth, nb, D_SSD, D_STATE)
    sconv_all = state_conv.reshape(depth, nb, -1)
    spool_all = state_pool.reshape(depth, nb, -1)

    ssm_p = ssm_s = conv_s = pool_s = None
    conv_p, pool_p = [], []
    for i in range(depth):
        x = _ffn_call(x, i, *f1, tm)
        xm, ssm_p, cp, plp = _mixp_call(x, i, mw, depth, batch, seq, L, ssm_p)
        z, yp, ypart, ea, xw8, cd38, c8, b8, conv_s, pool_s = _sa_call(
            x, sconv_all, spool_all, i, mw, depth, nb, nt, conv_s, pool_s)
        ssm_s, ch8 = _sb_call(h_all, c8, b8, xw8, cd38, i, depth, nb, sblk, ssm_s)
        x = _sc_call(x, z, ypart, ea, ch8, yp, i, mw, xm, nb, nt)
        x = _ffn_call(x, i, *f2, tm, ple=(p,) + ple_w, final=i == depth - 1)
        conv_p.append(cp)
        pool_p.append(plp)
    y_prompt = x[:n_prompt].reshape(batch, seq, d)
    y_sample = jnp.transpose(x[n_prompt:].reshape(nt, nb, d), (1, 0, 2))
    return (y_prompt, y_sample,
            ssm_p.reshape(depth, batch, N_HEADS, HEAD_DIM, D_STATE), jnp.stack(conv_p), jnp.stack(pool_p),
            ssm_s.reshape(depth, nb, N_HEADS, HEAD_DIM, D_STATE),
            conv_s.reshape(depth, nb, CONV_W - 1, CONV_DIM), pool_s.reshape(depth, nb, POOL_BUF, POOL_W))
```

```python
import functools

import jax
import jax.numpy as jnp
from jax import lax
from jax.experimental import pallas as pl
from jax.experimental.pallas import tpu as pltpu

F32 = jnp.float32
BF16 = jnp.bfloat16

EPS = 1e-6
D_MODEL = 1024
D_SSD = 1024
HEAD_DIM = 64
N_HEADS = 16
N_GROUPS = 2
HEADS_PER_GROUP = 8
D_STATE = 128
CONV_W = 4
CONV_DIM = D_SSD + 2 * N_GROUPS * D_STATE
CHUNK = 128
POOL_WINDOWS = (2, 4, 8, 16)
POOL_GD = 256
POOL_BUF = 15
POOL_W = 1024
LANES = 128
O_Z, O_XBC, O_XP, O_DT, O_END = 0, 1024, 2560, 3584, 3712
VMEM_LIMIT = 56 * 1024 * 1024
MIX_TILE = 256
DEC_SEQ_BLOCK = 8
DEC_TOKEN_SEQ_BLOCK = 32


def _dot(a, b):
    return jnp.dot(a, b, preferred_element_type=F32)


def _dot_t0(a, b):
    return lax.dot_general(a, b, (((0,), (0,)), ((), ())), preferred_element_type=F32)


def _dot_t1(a, b):
    return lax.dot_general(a, b, (((1,), (1,)), ((), ())), preferred_element_type=F32)


def _rms(x, w):
    return x * lax.rsqrt(jnp.mean(x * x, axis=-1, keepdims=True) + EPS) * w


def _silu(x):
    return x * jax.nn.sigmoid(x)


def _softplus(x):
    return jnp.maximum(x, 0.0) + jnp.log1p(jnp.exp(-jnp.abs(x)))


def _split3(v):
    hi = v.astype(BF16)
    r1 = v - hi.astype(F32)
    mid = r1.astype(BF16)
    lo = (r1 - mid.astype(F32)).astype(BF16)
    return jnp.concatenate([hi, mid, lo], axis=1)


def _expand_heads(v, e3):
    return _dot(_split3(v), e3)


FF_CHUNK = 512


def _ffn_kernel(*refs, ple, final, n_main, split_in, split_out):
    refs = list(refs)
    is_main = pl.program_id(0) < n_main
    if split_in:
        xa_ref, xb_ref = refs[:2]
        x = jnp.where(is_main, xa_ref[...], xb_ref[...])
        refs = refs[2:]
    else:
        x = refs[0][...]
        refs = refs[1:]
    nw_ref, wg_ref, wu_ref, wd_ref = refs[:4]
    refs = refs[4:]
    if ple:
        pa_ref, pb_ref, npw_ref, pg_ref, pp_ref, fn_ref = refs[:6]
        refs = refs[6:]
    u = _rms(x, nw_ref[...]).astype(BF16)
    ff = wg_ref.shape[-1]
    acc = None
    for c0 in range(0, ff, FF_CHUNK):
        c1 = min(c0 + FF_CHUNK, ff)
        g = _dot(u, wg_ref[:, c0:c1])
        up = _dot(u, wu_ref[:, c0:c1])
        d = _dot((_silu(g) * up).astype(BF16), wd_ref[c0:c1, :])
        acc = d if acc is None else acc + d
    x1 = x + 0.5 * acc
    if ple:
        un = _rms(x1, npw_ref[...]).astype(BF16)
        gate = jax.nn.sigmoid(_dot(un, pg_ref[...]))
        p = jnp.where(is_main, pa_ref[...], pb_ref[...]).astype(BF16)
        x1 = x1 + gate * _dot(p, pp_ref[...])
        if final:
            x1 = _rms(x1, fn_ref[...])
    if split_out:
        oa_ref, ob_ref = refs

        @pl.when(is_main)
        def _():
            oa_ref[...] = x1

        @pl.when(jnp.logical_not(is_main))
        def _():
            ob_ref[...] = x1
    else:
        refs[0][...] = x1


def _resident(shape, layer=None):
    n = len(shape)
    if layer is None:
        return pl.BlockSpec(shape, lambda *_: (0,) * n, pipeline_mode=pl.Buffered(1))
    return pl.BlockSpec((None,) + shape, lambda *_: (layer,) + (0,) * n, pipeline_mode=pl.Buffered(1))


def _ffn_call(x, layer, nw, wg, wu, wd, tm, n_main, ple=None, final=False, split_out=False):
    split_in = isinstance(x, tuple)
    d = wg.shape[1]
    ff = wg.shape[-1]

    def main_blk(i):
        return (jnp.minimum(i, n_main - 1), 0)

    def tail_blk(i):
        return (0, 0)

    if split_in:
        assert x[0].shape == (n_main * tm, d) and x[1].shape == (tm, d)
        in_specs = [pl.BlockSpec((tm, d), main_blk), pl.BlockSpec((tm, d), tail_blk)]
        args = list(x)
    else:
        assert x.shape == ((n_main + 1) * tm, d)
        in_specs = [pl.BlockSpec((tm, d), lambda i: (i, 0))]
        args = [x]
    in_specs += [_resident((1, d), layer), _resident((d, ff), layer), _resident((d, ff), layer),
                 _resident((ff, d), layer)]
    args += [nw, wg, wu, wd]
    if ple is not None:
        pa, pb, npw, pg, pp, fn = ple
        pd = pa.shape[-1]
        in_specs += [
            pl.BlockSpec((None, tm, pd), lambda i: (layer, jnp.minimum(i, n_main - 1), 0)),
            pl.BlockSpec((None, tm, pd), lambda i: (layer, 0, 0)),
            _resident((1, d), layer),
            _resident((d, d), layer),
            _resident((pd, d), layer),
            _resident((1, d)),
        ]
        args += [pa, pb, npw, pg, pp, fn]
    if split_out:
        out_specs = [pl.BlockSpec((tm, d), main_blk), pl.BlockSpec((tm, d), tail_blk)]
        out_shape = [jax.ShapeDtypeStruct((n_main * tm, d), F32), jax.ShapeDtypeStruct((tm, d), F32)]
    else:
        out_specs = pl.BlockSpec((tm, d), lambda i: (i, 0))
        out_shape = jax.ShapeDtypeStruct(((n_main + 1) * tm, d), F32)
    return pl.pallas_call(
        functools.partial(_ffn_kernel, ple=ple is not None, final=final, n_main=n_main, split_in=split_in,
                          split_out=split_out),
        grid=(n_main + 1,),
        in_specs=in_specs,
        out_specs=out_specs,
        out_shape=out_shape,
        compiler_params=pltpu.CompilerParams(
            dimension_semantics=("arbitrary",), vmem_limit_bytes=VMEM_LIMIT),
        name="ffn_ple" if ple is not None else "ffn",
    )(*args)


PROJ_CHUNK = 256


def _proj_chunks(w_refs):
    out, dst = [], 0
    for w_ref in w_refs:
        width = w_ref.shape[0]
        for lo in range(0, width, PROJ_CHUNK):
            hi = min(lo + PROJ_CHUNK, width)
            out.append((w_ref, lo, hi, dst + lo))
        dst += width
    return out


def _in_proj(x_ref, nw_ref, w_refs, proj_ref):
    u = _rms(x_ref[...], nw_ref[...]).astype(BF16)
    for w_ref, lo, hi, dst in _proj_chunks(w_refs):
        proj_ref[:, dst:dst + hi - lo] = _dot_t1(u, w_ref[lo:hi, :])


class _Filler:
    def __init__(self, u_ref, w_refs, dst_ref):
        self.u_ref, self.dst_ref = u_ref, dst_ref
        self.todo = _proj_chunks(w_refs)

    def emit(self, n=1):
        for _ in range(n):
            if self.todo:
                w_ref, lo, hi, dst = self.todo.pop(0)
                self.dst_ref[:, dst:dst + hi - lo] = _dot_t1(self.u_ref[...], w_ref[lo:hi, :])

    def drain(self):
        self.emit(len(self.todo))


def _ssd_chunk(c, dt_sc, xbc_sc, y_sc, h_sc, a_row, dsk, e3, tril, causal, lo_mask, filler):
    rows = slice(c * CHUNK, (c + 1) * CHUNK)
    dtc = dt_sc[rows, :]
    acum = _dot(tril, _split3(dtc * a_row))
    acum = acum[:, 0:LANES] + acum[:, LANES:2 * LANES] + acum[:, 2 * LANES:3 * LANES]
    a_t = acum.T[0:N_HEADS, :]
    dt_t = dtc.T[0:N_HEADS, :]
    w_t = dt_t * jnp.exp(a_t[:, CHUNK - 1:CHUNK] - a_t)
    cd = jnp.exp(_expand_heads(acum[CHUNK - 8:CHUNK, :], e3)[7:8, :])
    for g in range(N_GROUPS):
        b_g = xbc_sc[rows, D_SSD + g * D_STATE:D_SSD + (g + 1) * D_STATE]
        c_g = xbc_sc[rows, D_SSD + (N_GROUPS + g) * D_STATE:D_SSD + (N_GROUPS + g + 1) * D_STATE]
        bt_g = b_g.T
        s_g = _dot(c_g.astype(BF16), bt_g.astype(BF16))
        for k in range(HEADS_PER_GROUP // 2):
            lb = g * (HEADS_PER_GROUP // 2) + k
            cols = slice(lb * LANES, (lb + 1) * LANES)
            lhs, btw = [], []
            for r in (2 * lb, 2 * lb + 1):
                lq = acum[:, r:r + 1]
                dec = jnp.where(causal, jnp.exp(lq - a_t[r:r + 1, :]), 0.0)
                lhs.append((s_g * dec * dt_t[r:r + 1, :]).astype(BF16))
                lhs.append((c_g * jnp.exp(lq)).astype(BF16))
                btw.append((bt_g * w_t[r:r + 1, :]).astype(BF16))
            xs = xbc_sc[rows, cols]
            hp = h_sc[:, cols]
            x_lo = jnp.where(lo_mask, xs, 0.0).astype(BF16)
            x_hi = jnp.where(lo_mask, 0.0, xs).astype(BF16)
            h_lo = jnp.where(lo_mask, hp, 0.0).astype(BF16)
            h_hi = jnp.where(lo_mask, 0.0, hp).astype(BF16)
            y = _dot(jnp.concatenate(lhs, axis=1), jnp.concatenate([x_lo, h_lo, x_hi, h_hi], axis=0))
            y_sc[rows, cols] = y + dsk[:, cols] * xs
            upd = _dot(jnp.concatenate(btw, axis=1), jnp.concatenate([x_lo, x_hi], axis=0))
            h_sc[:, cols] = hp * cd[:, cols] + upd
            filler.emit()


def _pool_group_sums(ext, gi, w):
    s = ext[:, gi * POOL_GD:(gi + 1) * POOL_GD]
    sh = 1
    while sh < w:
        s = s + pltpu.roll(s, sh, 0)
        sh *= 2
    return s[16:, :]


def _mixp_kernel(x_ref, xn_ref, nw_ref, wzx_ref, wxp_ref, wdt_ref, cw_ref, cb_ref, dtb_ref, alog_ref, dsk_ref,
                 snw_ref, pw_ref, ps_ref, wout_ref, e3_ref,
                 xo_ref, ssm_ref, conv_ref, pool_ref,
                 h_sc, cext_sc, pext_sc, xbc_sc, dt_sc, y_sc, u_sc, proj_a, proj_b, *, L, n_tiles):
    i = pl.program_id(1)
    g = pl.program_id(0) * n_tiles + i
    w_refs = (wzx_ref, wxp_ref, wdt_ref)

    @pl.when(g == 0)
    def _():
        _in_proj(x_ref, nw_ref, w_refs, proj_a)

    @pl.when(i == 0)
    def _():
        h_sc[...] = jnp.zeros_like(h_sc)
        cext_sc[0:8, :] = jnp.zeros((8, CONV_DIM), F32)
        pext_sc[0:16, :] = jnp.zeros((16, POOL_W), F32)

    tile = functools.partial(
        _mixp_tile, i, x_ref, xn_ref, nw_ref, w_refs, cw_ref, cb_ref, dtb_ref, alog_ref, dsk_ref, snw_ref, pw_ref,
        ps_ref, wout_ref, e3_ref, xo_ref, h_sc, cext_sc, pext_sc, xbc_sc, dt_sc, y_sc, u_sc, L=L)

    @pl.when(g % 2 == 0)
    def _():
        tile(proj_a, proj_b)

    @pl.when(g % 2 == 1)
    def _():
        tile(proj_b, proj_a)

    @pl.when(i == n_tiles - 1)
    def _():
        ssm_ref[...] = h_sc[...].T
        conv_ref[...] = cext_sc[L + 5:L + 8, :]
        pool_ref[...] = pext_sc[L + 1:L + 16, :]

    cext_sc[0:8, :] = cext_sc[L:L + 8, :]
    pext_sc[0:16, :] = pext_sc[L:L + 16, :]


def _mixp_tile(i, x_ref, xn_ref, nw_ref, w_refs, cw_ref, cb_ref, dtb_ref, alog_ref, dsk_ref, snw_ref, pw_ref,
               ps_ref, wout_ref, e3_ref, xo_ref, h_sc, cext_sc, pext_sc, xbc_sc, dt_sc, y_sc, u_sc, proj_cur, proj_nxt,
               *, L):
    u_sc[...] = _rms(xn_ref[...], nw_ref[...]).astype(BF16)
    filler = _Filler(u_sc, w_refs, proj_nxt)

    cext_sc[8:8 + L, :] = proj_cur[:, O_XBC:O_XP]
    cw = cw_ref[...]
    cb = cb_ref[...]
    for c0 in range(0, CONV_DIM, PROJ_CHUNK):
        cs = slice(c0, c0 + PROJ_CHUNK)
        conv = cb[:, cs] + cw[3:4, cs] * cext_sc[8:8 + L, cs]
        for k in range(CONV_W - 1):
            conv = conv + cw[k:k + 1, cs] * cext_sc[5 + k:5 + k + L, cs]
        xbc_sc[:, cs] = _silu(conv)
        filler.emit()

    dt_sc[...] = _softplus(proj_cur[:, O_DT:O_END] + dtb_ref[...])
    a_row = -jnp.exp(alog_ref[...])
    filler.emit()

    ri = lax.broadcasted_iota(jnp.int32, (CHUNK, CHUNK), 0)
    ci = lax.broadcasted_iota(jnp.int32, (CHUNK, CHUNK), 1)
    causal = ri >= ci
    tril = causal.astype(BF16)
    lo_mask = ci < HEAD_DIM
    dsk = dsk_ref[...]
    e3 = e3_ref[...]
    xp = proj_cur[:, O_XP:O_DT]
    pext_sc[16:16 + L, :] = xp
    ext = pext_sc[...]
    t_abs = i * L + lax.broadcasted_iota(jnp.int32, (L, 1), 0)
    yps = []
    for gi, w in enumerate(POOL_WINDOWS):
        cnt = jnp.minimum(t_abs + 1, w).astype(F32)
        d = _pool_group_sums(ext, gi, w) / cnt - xp[:, gi * POOL_GD:(gi + 1) * POOL_GD]
        yps.append(_dot(d.astype(BF16), pw_ref[gi]))
    yp = (jnp.concatenate(yps, axis=1) * ps_ref[...]).astype(BF16)
    out_pool = _dot(yp, wout_ref[D_SSD:D_SSD + POOL_W, :])

    for c in range(L // CHUNK):
        _ssd_chunk(c, dt_sc, xbc_sc, y_sc, h_sc, a_row, dsk, e3, tril, causal, lo_mask, filler)
    filler.drain()

    yn = _rms(y_sc[...] * _silu(proj_cur[:, O_Z:O_XBC]), snw_ref[...]).astype(BF16)
    xo_ref[...] = x_ref[...] + (out_pool + _dot(yn, wout_ref[0:D_SSD, :]))


def _mixp_call(x, layer, w, depth, batch, seq, L, ssm_prev):
    n_tiles = seq // L
    d = x.shape[-1]
    last = batch * n_tiles - 1
    in_specs = [
        pl.BlockSpec((L, d), lambda b, i: (b * n_tiles + i, 0)),
        pl.BlockSpec((L, d), lambda b, i: (jnp.minimum(b * n_tiles + i + 1, last), 0)),
        _resident((1, d), layer),
        _resident((O_XP, d), layer),
        _resident((O_DT - O_XP, d), layer),
        _resident((O_END - O_DT, d), layer),
        _resident((CONV_W, CONV_DIM), layer),
        _resident((1, CONV_DIM), layer),
        _resident((1, LANES), layer),
        _resident((1, LANES), layer),
        _resident((1, D_SSD), layer),
        _resident((1, D_SSD), layer),
        _resident((len(POOL_WINDOWS), POOL_GD, POOL_GD), layer),
        _resident((1, POOL_W), layer),
        _resident((D_SSD + POOL_W, d), layer),
        _resident((3 * LANES, D_SSD)),
    ]
    args = [x, x, w["norm_mix"], w["w_zx"], w["w_xp"], w["w_dt"], w["conv_w"], w["conv_b"], w["dt_bias"],
            w["a_log"], w["d_skip"], w["ssd_norm_w"], w["pool_w"], w["pool_scale"], w["w_out"], w["e3"]]
    n_in = len(args)
    aliases = {}
    if ssm_prev is not None:
        in_specs.append(pl.BlockSpec(memory_space=pl.ANY))
        aliases = {n_in: 1}
        args.append(ssm_prev)
    out_specs = [
        pl.BlockSpec((L, d), lambda b, i: (b * n_tiles + i, 0)),
        pl.BlockSpec((None, None, D_SSD, D_STATE), lambda b, i: (layer, b, 0, 0)),
        pl.BlockSpec((None, CONV_W - 1, CONV_DIM), lambda b, i: (b, 0, 0)),
        pl.BlockSpec((None, POOL_BUF, POOL_W), lambda b, i: (b, 0, 0)),
    ]
    out_shape = [
        jax.ShapeDtypeStruct((batch * seq, d), F32),
        jax.ShapeDtypeStruct((depth, batch, D_SSD, D_STATE), F32),
        jax.ShapeDtypeStruct((batch, CONV_W - 1, CONV_DIM), F32),
        jax.ShapeDtypeStruct((batch, POOL_BUF, POOL_W), F32),
    ]
    scratch = [
        pltpu.VMEM((D_STATE, D_SSD), F32),
        pltpu.VMEM((8 + L, CONV_DIM), F32),
        pltpu.VMEM((16 + L, POOL_W), F32),
        pltpu.VMEM((L, CONV_DIM), F32),
        pltpu.VMEM((L, LANES), F32),
        pltpu.VMEM((L, D_SSD), F32),
        pltpu.VMEM((L, D_MODEL), BF16),
        pltpu.VMEM((L, O_END), F32),
        pltpu.VMEM((L, O_END), F32),
    ]

    def body(*refs):
        if ssm_prev is not None:
            refs = refs[:n_in] + refs[n_in + 1:]
        _mixp_kernel(*refs, L=L, n_tiles=n_tiles)

    return pl.pallas_call(
        body,
        grid=(batch, n_tiles),
        in_specs=in_specs,
        out_specs=out_specs,
        out_shape=out_shape,
        scratch_shapes=scratch,
        input_output_aliases=aliases,
        compiler_params=pltpu.CompilerParams(
            dimension_semantics=("arbitrary", "arbitrary"), vmem_limit_bytes=VMEM_LIMIT),
        name="mixer_prompt",
    )(*args)


def _store_seq8(ref, t, val):
    nb = val.shape[0]
    for k in range(ref.shape[0]):
        ref[k, pl.ds(t, nb, stride=8), :] = val[:, k * LANES:(k + 1) * LANES]


def _load_rows(ref, rows):
    return jnp.concatenate([ref[k, rows, :] for k in range(ref.shape[0])], axis=1)


def _sa_kernel(x_ref, sconv_ref, spool_ref, nw_ref, wzx_ref, wxp_ref, wdt_ref, cw_ref, cb_ref, dtb_ref, alog_ref,
               dsk_ref, pw_ref, ps_ref, e3_ref,
               z_ref, yp_ref, ypart_ref, ea_ref, xw_ref, cd3_ref, c_ref, b_ref, conv_ref, pool_ref,
               *, nb, nt):
    u = _rms(x_ref[...].reshape(nt * nb, -1), nw_ref[...]).astype(BF16)
    z_ref[...] = _dot_t1(u, wzx_ref[O_Z:O_XBC, :]).reshape(nt, nb, D_SSD)
    xbc = _dot_t1(u, wzx_ref[O_XBC:O_XP, :])
    xp = _dot_t1(u, wxp_ref[...])
    dtr = _dot_t1(u, wdt_ref[...])
    e3 = e3_ref[...]

    def tile(v, t):
        return v[t * nb:(t + 1) * nb, :]

    cext = [sconv_ref[:, k, :] for k in range(CONV_W - 1)]
    cext += [tile(xbc, t) for t in range(nt)]
    cw = cw_ref[...]
    xs, bm, cm = [], [], []
    for t in range(nt):
        acc = cb_ref[...]
        for k in range(CONV_W):
            acc = acc + cw[k:k + 1, :] * cext[t + k]
        v = _silu(acc)
        xs.append(v[:, 0:D_SSD])
        bm.append(v[:, D_SSD:D_SSD + N_GROUPS * D_STATE])
        cm.append(v[:, D_SSD + N_GROUPS * D_STATE:])
    for k in range(CONV_W - 1):
        conv_ref[:, k, :] = cext[nt + k]

    dt = _softplus(dtr + dtb_ref[...])
    a_row = -jnp.exp(alog_ref[...])
    dts = [tile(dt, t) for t in range(nt)]
    acum = []
    for t in range(nt):
        da = dts[t] * a_row
        acum.append(da if t == 0 else acum[-1] + da)
    a_last = acum[-1]

    for r in (xw_ref, cd3_ref, c_ref, b_ref):
        r[...] = jnp.zeros(r.shape, F32)
    lane = lax.broadcasted_iota(jnp.int32, (nb, LANES), 1)
    g0_heads = lane < HEADS_PER_GROUP
    dsk = dsk_ref[...]
    for t in range(nt):
        y = dsk * xs[t]
        for s in range(t + 1):
            sc = []
            for g in range(N_GROUPS):
                cg = cm[t][:, g * D_STATE:(g + 1) * D_STATE]
                bg = bm[s][:, g * D_STATE:(g + 1) * D_STATE]
                sc.append(jnp.sum(cg * bg, axis=1, keepdims=True))
            wts = jnp.where(g0_heads, sc[0], sc[1]) * jnp.exp(acum[t] - acum[s]) * dts[s]
            y = y + _expand_heads(wts, e3) * xs[s]
        ypart_ref[t] = y
        ea_ref[t] = _expand_heads(jnp.exp(acum[t]), e3)
        _store_seq8(xw_ref, t, _expand_heads(dts[t] * jnp.exp(a_last - acum[t]), e3) * xs[t])
        _store_seq8(c_ref, t, cm[t])
        _store_seq8(b_ref, t, bm[t])
    cd = jnp.exp(_expand_heads(a_last, e3))
    hi = cd.astype(BF16).astype(F32)
    mid = (cd - hi).astype(BF16).astype(F32)
    lo = (cd - hi - mid).astype(BF16).astype(F32)
    _store_seq8(cd3_ref, 0, hi)
    _store_seq8(cd3_ref, 1, mid)
    _store_seq8(cd3_ref, 2, lo)

    pext = [spool_ref[:, k, :] for k in range(POOL_BUF)]
    pext += [tile(xp, t) for t in range(nt)]
    for t in range(nt):
        yps = []
        for gi, w in enumerate(POOL_WINDOWS):
            cols = slice(gi * POOL_GD, (gi + 1) * POOL_GD)
            s = pext[POOL_BUF + t][:, cols]
            for k in range(1, w):
                s = s + pext[POOL_BUF + t - k][:, cols]
            d = s / float(w) - pext[POOL_BUF + t][:, cols]
            yps.append(_dot(d.astype(BF16), pw_ref[gi]))
        yp_ref[t] = (jnp.concatenate(yps, axis=1) * ps_ref[...]).astype(BF16)
    for k in range(POOL_BUF):
        pool_ref[:, k, :] = pext[nt + k]


def _sa_call(x_dec, sconv, spool, layer, w, depth, sb, conv_prev, pool_prev):
    nt, nb, d = x_dec.shape
    in_specs = [
        pl.BlockSpec((nt, sb, d), lambda i: (0, i, 0)),
        pl.BlockSpec((None, sb, CONV_W - 1, CONV_DIM), lambda i: (layer, i, 0, 0)),
        pl.BlockSpec((None, sb, POOL_BUF, POOL_W), lambda i: (layer, i, 0, 0)),
        _resident((1, d), layer),
        _resident((O_XP, d), layer),
        _resident((O_DT - O_XP, d), layer),
        _resident((O_END - O_DT, d), layer),
        _resident((CONV_W, CONV_DIM), layer),
        _resident((1, CONV_DIM), layer),
        _resident((1, LANES), layer),
        _resident((1, LANES), layer),
        _resident((1, D_SSD), layer),
        _resident((len(POOL_WINDOWS), POOL_GD, POOL_GD), layer),
        _resident((1, POOL_W), layer),
        _resident((3 * LANES, D_SSD)),
    ]
    args = [x_dec, sconv, spool, w["norm_mix"], w["w_zx"], w["w_xp"], w["w_dt"], w["conv_w"], w["conv_b"],
            w["dt_bias"], w["a_log"], w["d_skip"], w["pool_w"], w["pool_scale"], w["e3"]]
    n_in = len(args)
    aliases = {}
    if conv_prev is not None:
        in_specs += [pl.BlockSpec(memory_space=pl.ANY), pl.BlockSpec(memory_space=pl.ANY)]
        aliases = {n_in: 8, n_in + 1: 9}
        args += [conv_prev, pool_prev]
    gn = N_GROUPS * D_STATE
    tm_outs = [(D_SSD, F32), (POOL_W, BF16), (D_SSD, F32), (D_SSD, F32)]
    out_specs = [pl.BlockSpec((nt, sb, wd), lambda i: (0, i, 0)) for wd, _ in tm_outs]
    out_shape = [jax.ShapeDtypeStruct((nt, nb, wd), dt) for wd, dt in tm_outs]
    for wd in (D_SSD, D_SSD, gn, gn):
        out_specs.append(pl.BlockSpec((wd // LANES, sb * 8, LANES), lambda i: (0, i, 0)))
        out_shape.append(jax.ShapeDtypeStruct((wd // LANES, nb * 8, LANES), F32))
    for n_rows, wd in ((CONV_W - 1, CONV_DIM), (POOL_BUF, POOL_W)):
        out_specs.append(pl.BlockSpec((None, sb, n_rows, wd), lambda i: (layer, i, 0, 0)))
        out_shape.append(jax.ShapeDtypeStruct((depth, nb, n_rows, wd), F32))

    def body(*refs):
        if conv_prev is not None:
            refs = refs[:n_in] + refs[n_in + 2:]
        _sa_kernel(*refs, nb=sb, nt=nt)

    return pl.pallas_call(
        body,
        grid=(nb // sb,),
        in_specs=in_specs,
        out_specs=out_specs,
        out_shape=out_shape,
        input_output_aliases=aliases,
        compiler_params=pltpu.CompilerParams(
            dimension_semantics=("arbitrary",), vmem_limit_bytes=VMEM_LIMIT),
        name="mixer_decode_tokens",
    )(*args)


def _sb_kernel(h_ref, c_ref, b_ref, xw_ref, cd3_ref, hn_ref, ch_ref, *, sblk):
    half = D_SSD // N_GROUPS
    ones = jnp.ones((8, D_STATE), BF16)
    for j in range(sblk):
        rows = slice(j * 8, (j + 1) * 8)
        h0 = h_ref[j]
        hb = h0.astype(BF16)
        cj = _load_rows(c_ref, rows).astype(BF16)
        bj = _load_rows(b_ref, rows).astype(BF16)
        xw = _load_rows(xw_ref, rows).astype(BF16)
        dmat = _dot_t0(_load_rows(cd3_ref, rows).astype(BF16), ones)
        chs, upds = [], []
        for g in range(N_GROUPS):
            rs = slice(g * half, (g + 1) * half)
            ns = slice(g * D_STATE, (g + 1) * D_STATE)
            chs.append(_dot_t1(cj[:, ns], hb[rs, :]))
            upds.append(_dot_t0(xw[:, rs], bj[:, ns]))
        ch = jnp.concatenate(chs, axis=1)
        for k in range(ch_ref.shape[0]):
            ch_ref[k, rows, :] = ch[:, k * LANES:(k + 1) * LANES]
        hn_ref[j] = h0 * dmat + jnp.concatenate(upds, axis=0)


def _sb_call(h0, c8, b8, xw8, cd38, layer, depth, nb, sblk, ssm_prev):
    gn = N_GROUPS * D_STATE
    in_specs = [
        pl.BlockSpec((None, sblk, D_SSD, D_STATE), lambda i: (layer, i, 0, 0)),
        pl.BlockSpec((gn // LANES, sblk * 8, LANES), lambda i: (0, i, 0)),
        pl.BlockSpec((gn // LANES, sblk * 8, LANES), lambda i: (0, i, 0)),
        pl.BlockSpec((D_SSD // LANES, sblk * 8, LANES), lambda i: (0, i, 0)),
        pl.BlockSpec((D_SSD // LANES, sblk * 8, LANES), lambda i: (0, i, 0)),
    ]
    args = [h0, c8, b8, xw8, cd38]
    n_in = len(args)
    aliases = {}
    if ssm_prev is not None:
        in_specs.append(pl.BlockSpec(memory_space=pl.ANY))
        aliases = {n_in: 0}
        args.append(ssm_prev)

    def body(*refs):
        if ssm_prev is not None:
            refs = refs[:n_in] + refs[n_in + 1:]
        _sb_kernel(*refs, sblk=sblk)

    return pl.pallas_call(
        body,
        grid=(nb // sblk,),
        in_specs=in_specs,
        out_specs=[
            pl.BlockSpec((None, sblk, D_SSD, D_STATE), lambda i: (layer, i, 0, 0)),
            pl.BlockSpec((D_SSD // LANES, sblk * 8, LANES), lambda i: (0, i, 0)),
        ],
        out_shape=[
            jax.ShapeDtypeStruct((depth, nb, D_SSD, D_STATE), F32),
            jax.ShapeDtypeStruct((D_SSD // LANES, nb * 8, LANES), F32),
        ],
        input_output_aliases=aliases,
        compiler_params=pltpu.CompilerParams(
            dimension_semantics=("parallel",), vmem_limit_bytes=VMEM_LIMIT),
        name="mixer_decode_state",
    )(*args)


def _sc_kernel(x_ref, z_ref, ypart_ref, ea_ref, ch_ref, yp_ref, snw_ref, wout_ref, o_ref, *, nb, nt):
    ch =jnp.concatenate([_load_rows(ch_ref, pl.ds(t, nb, stride=8)) for t in range(nt)], axis=0)
    y = ypart_ref[...] + ea_ref[...] * ch
    yn = _rms(y * _silu(z_ref[...]), snw_ref[...]).astype(BF16)
    out = _dot(jnp.concatenate([yn, yp_ref[...]], axis=1), wout_ref[...])
    o_ref[...] = x_ref[...] + out


def _sc_call(x, z, ypart, ea, ch8, yp, layer, w, nb, nt):
    t_all, d = x.shape
    rows = nt * nb
    xblk = (t_all - rows) // rows

    def full(shape):
        return pl.BlockSpec(shape, lambda i: (0, 0))

    return pl.pallas_call(
        functools.partial(_sc_kernel, nb=nb, nt=nt),
        grid=(1,),
        in_specs=[pl.BlockSpec((rows, d), lambda i: (xblk, 0)), full((rows, D_SSD)), full((rows, D_SSD)),
                  full((rows, D_SSD)), pl.BlockSpec((D_SSD // LANES, nb * 8, LANES), lambda i: (0, 0, 0)),
                  full((rows, POOL_W)),
                  _resident((1, D_SSD), layer), _resident((D_SSD + POOL_W, d), layer)],
        out_specs=pl.BlockSpec((rows, d), lambda i: (0, 0)),
        out_shape=jax.ShapeDtypeStruct((rows, d), F32),
        compiler_params=pltpu.CompilerParams(vmem_limit_bytes=VMEM_LIMIT),
        name="mixer_decode_out",
    )(x, z, ypart, ea, ch8, yp, w["ssd_norm_w"], w["w_out"])


def _prep_weights(w_in, conv_w, conv_b, dt_bias, a_log, d_skip, ssd_norm_w, pool_w, pool_scale, w_out,
                  norm_mix):
    depth = w_in.shape[0]
    o1, o2, o3 = D_SSD, D_SSD + CONV_DIM, D_SSD + CONV_DIM + N_HEADS
    w_t = jnp.transpose(w_in, (0, 2, 1))
    w_dt = jnp.pad(w_t[:, o2:o3], ((0, 0), (0, LANES - N_HEADS), (0, 0)))
    pad_h = ((0, 0), (0, LANES - N_HEADS))
    head_of_lane = jnp.arange(D_SSD) // HEAD_DIM
    e1 = (jnp.arange(LANES)[:, None] == head_of_lane[None, :]).astype(BF16)
    return {
        "norm_mix": norm_mix.reshape(depth, 1, -1),
        "w_zx": w_t[:, :o2].astype(BF16),
        "w_xp": w_t[:, o3:].astype(BF16),
        "w_dt": w_dt.astype(BF16),
        "conv_w": conv_w,
        "conv_b": conv_b.reshape(depth, 1, -1),
        "dt_bias": jnp.pad(dt_bias, pad_h).reshape(depth, 1, LANES),
        "a_log": jnp.pad(a_log, pad_h).reshape(depth, 1, LANES),
        "d_skip": jnp.repeat(d_skip, HEAD_DIM, axis=-1).reshape(depth, 1, D_SSD),
        "ssd_norm_w": ssd_norm_w.reshape(depth, 1, -1),
        "pool_w": pool_w.astype(BF16),
        "pool_scale": pool_scale.reshape(depth, 1, -1),
        "w_out": w_out.astype(BF16),
        "e3": jnp.concatenate([e1, e1, e1], axis=0),
    }


def kernel(x_prompt, x_sample, p_prompt, p_sample, state_ssm, state_conv, state_pool, w_in, conv_w, conv_b,
           dt_bias, a_log, d_skip, ssd_norm_w, pool_w, pool_scale, w_out, norm_ffn1, ffn1_gate, ffn1_up,
           ffn1_down, norm_mix, norm_ffn2, ffn2_gate, ffn2_up, ffn2_down, norm_ple, ple_gate, ple_proj,
           final_norm):
    L, sblk = MIX_TILE, DEC_SEQ_BLOCK
    depth = w_in.shape[0]
    batch, seq, d = x_prompt.shape
    nb, nt, _ = x_sample.shape
    n_prompt, n_dec = batch * seq, nt * nb
    tm = n_dec
    n_main = n_prompt // tm
    assert n_main * tm == n_prompt
    mw = _prep_weights(w_in, conv_w, conv_b, dt_bias, a_log, d_skip, ssd_norm_w, pool_w, pool_scale, w_out,
                       norm_mix)
    f1 = (norm_ffn1.reshape(depth, 1, d), ffn1_gate.astype(BF16), ffn1_up.astype(BF16), ffn1_down.astype(BF16))
    f2 = (norm_ffn2.reshape(depth, 1, d), ffn2_gate.astype(BF16), ffn2_up.astype(BF16), ffn2_down.astype(BF16))
    ple = (p_prompt.reshape(depth, n_prompt, -1), jnp.transpose(p_sample, (0, 2, 1, 3)).reshape(depth, n_dec, -1),
           norm_ple.reshape(depth, 1, d), ple_gate.astype(BF16), ple_proj.astype(BF16), final_norm.reshape(1, d))
    x = (x_prompt.reshape(n_prompt, d), jnp.transpose(x_sample, (1, 0, 2)).reshape(n_dec, d))
    h_all = state_ssm.reshape(depth, nb, D_SSD, D_STATE)

    ssm_p = ssm_s = conv_s = pool_s = None
    conv_p, pool_p = [], []
    for i in range(depth):
        last = i == depth - 1
        x = _ffn_call(x, i, *f1, tm, n_main)
        xm, ssm_p, cp, plp = _mixp_call(x, i, mw, depth, batch, seq, L, ssm_p)
        z, yp, ypart, ea, xw8, cd38, c8, b8, conv_s, pool_s = _sa_call(
            x[n_prompt:].reshape(nt, nb, d), state_conv, state_pool, i, mw, depth, DEC_TOKEN_SEQ_BLOCK, conv_s, pool_s)
        z, yp, ypart, ea = (v.reshape(n_dec, -1) for v in (z, yp, ypart, ea))
        ssm_s, ch8 = _sb_call(h_all, c8, b8, xw8, cd38, i, depth, nb, sblk, ssm_s)
        xd = _sc_call(x, z, ypart, ea, ch8, yp, i, mw, nb, nt)
        x = _ffn_call((xm, xd), i, *f2, tm, n_main, ple=ple, final=last, split_out=last)
        conv_p.append(cp)
        pool_p.append(plp)
    y_prompt = x[0].reshape(batch, seq, d)
    y_sample = jnp.transpose(x[1].reshape(nt, nb, d), (1, 0, 2))
    return (y_prompt, y_sample,
            ssm_p.reshape(depth, batch, N_HEADS, HEAD_DIM, D_STATE), jnp.stack(conv_p), jnp.stack(pool_p),
            ssm_s.reshape(depth, nb, N_HEADS, HEAD_DIM, D_STATE),
            conv_s, pool_s)
```

```python
import functools

import jax
import jax.numpy as jnp
from jax import lax
from jax.experimental import pallas as pl
from jax.experimental.pallas import tpu as pltpu

F32 = jnp.float32
BF16 = jnp.bfloat16

EPS = 1e-6
D_MODEL = 1024
D_SSD = 1024
HEAD_DIM = 64
N_HEADS = 16
N_GROUPS = 2
HEADS_PER_GROUP = 8
D_STATE = 128
CONV_W = 4
CONV_DIM = D_SSD + 2 * N_GROUPS * D_STATE
CHUNK = 128
POOL_WINDOWS = (2, 4, 8, 16)
POOL_GD = 256
POOL_BUF = 15
POOL_W = 1024
LANES = 128
O_Z, O_XBC, O_XP, O_DT, O_END = 0, 1024, 2560, 3584, 3712
VMEM_LIMIT = 56 * 1024 * 1024
MIX_TILE = 256
DEC_SEQ_BLOCK = 8
DEC_TOKEN_SEQ_BLOCK = 64


def _dot(a, b):
    return jnp.dot(a, b, preferred_element_type=F32)


def _dot_t0(a, b):
    return lax.dot_general(a, b, (((0,), (0,)), ((), ())), preferred_element_type=F32)


def _dot_t1(a, b):
    return lax.dot_general(a, b, (((1,), (1,)), ((), ())), preferred_element_type=F32)


def _rms(x, w):
    return x * lax.rsqrt(jnp.mean(x * x, axis=-1, keepdims=True) + EPS) * w


def _silu(x):
    return x * jax.nn.sigmoid(x)


def _softplus(x):
    return jnp.maximum(x, 0.0) + jnp.log1p(jnp.exp(-jnp.abs(x)))


def _split3(v):
    hi = v.astype(BF16)
    r1 = v - hi.astype(F32)
    mid = r1.astype(BF16)
    lo = (r1 - mid.astype(F32)).astype(BF16)
    return jnp.concatenate([hi, mid, lo], axis=1)


def _expand_heads(v, e3):
    return _dot(_split3(v), e3)


FF_CHUNK = 512


def _ffn_kernel(*refs, ple, final, n_main, split_in, split_out):
    refs = list(refs)
    is_main = pl.program_id(0) < n_main
    if split_in:
        xa_ref, xb_ref = refs[:2]
        x = jnp.where(is_main, xa_ref[...], xb_ref[...])
        refs = refs[2:]
    else:
        x = refs[0][...]
        refs = refs[1:]
    nw_ref, wg_ref, wu_ref, wd_ref = refs[:4]
    refs = refs[4:]
    if ple:
        pa_ref, pb_ref, npw_ref, pg_ref, pp_ref, fn_ref = refs[:6]
        refs = refs[6:]
    u = _rms(x, nw_ref[...]).astype(BF16)
    ff = wg_ref.shape[-1]
    acc = None
    for c0 in range(0, ff, FF_CHUNK):
        c1 = min(c0 + FF_CHUNK, ff)
        g = _dot(u, wg_ref[:, c0:c1])
        up = _dot(u, wu_ref[:, c0:c1])
        d = _dot((_silu(g) * up).astype(BF16), wd_ref[c0:c1, :])
        acc = d if acc is None else acc + d
    x1 = x + 0.5 * acc
    if ple:
        un = _rms(x1, npw_ref[...]).astype(BF16)
        gate = jax.nn.sigmoid(_dot(un, pg_ref[...]))
        p = jnp.where(is_main, pa_ref[...], pb_ref[...]).astype(BF16)
        x1 = x1 + gate * _dot(p, pp_ref[...])
        if final:
            x1 = _rms(x1, fn_ref[...])
    if split_out:
        oa_ref, ob_ref = refs

        @pl.when(is_main)
        def _():
            oa_ref[...] = x1

        @pl.when(jnp.logical_not(is_main))
        def _():
            ob_ref[...] = x1
    else:
        refs[0][...] = x1


def _resident(shape, layer=None):
    n = len(shape)
    if layer is None:
        return pl.BlockSpec(shape, lambda *_: (0,) * n, pipeline_mode=pl.Buffered(1))
    return pl.BlockSpec((None,) + shape, lambda *_: (layer,) + (0,) * n, pipeline_mode=pl.Buffered(1))


def _ffn_call(x, layer, nw, wg, wu, wd, tm, n_main, ple=None, final=False, split_out=False):
    split_in = isinstance(x, tuple)
    d = wg.shape[1]
    ff = wg.shape[-1]

    def main_blk(i):
        return (jnp.minimum(i, n_main - 1), 0)

    def tail_blk(i):
        return (0, 0)

    if split_in:
        assert x[0].shape == (n_main * tm, d) and x[1].shape == (tm, d)
        in_specs = [pl.BlockSpec((tm, d), main_blk), pl.BlockSpec((tm, d), tail_blk)]
        args = list(x)
    else:
        assert x.shape == ((n_main + 1) * tm, d)
        in_specs = [pl.BlockSpec((tm, d), lambda i: (i, 0))]
        args = [x]
    in_specs += [_resident((1, d), layer), _resident((d, ff), layer), _resident((d, ff), layer),
                 _resident((ff, d), layer)]
    args += [nw, wg, wu, wd]
    if ple is not None:
        pa, pb, npw, pg, pp, fn = ple
        pd = pa.shape[-1]
        in_specs += [
            pl.BlockSpec((None, tm, pd), lambda i: (layer, jnp.minimum(i, n_main - 1), 0)),
            pl.BlockSpec((None, tm, pd), lambda i: (layer, 0, 0)),
            _resident((1, d), layer),
            _resident((d, d), layer),
            _resident((pd, d), layer),
            _resident((1, d)),
        ]
        args += [pa, pb, npw, pg, pp, fn]
    if split_out:
        out_specs = [pl.BlockSpec((tm, d), main_blk), pl.BlockSpec((tm, d), tail_blk)]
        out_shape = [jax.ShapeDtypeStruct((n_main * tm, d), F32), jax.ShapeDtypeStruct((tm, d), F32)]
    else:
        out_specs = pl.BlockSpec((tm, d), lambda i: (i, 0))
        out_shape = jax.ShapeDtypeStruct(((n_main + 1) * tm, d), F32)
    return pl.pallas_call(
        functools.partial(_ffn_kernel, ple=ple is not None, final=final, n_main=n_main, split_in=split_in,
                          split_out=split_out),
        grid=(n_main + 1,),
        in_specs=in_specs,
        out_specs=out_specs,
        out_shape=out_shape,
        compiler_params=pltpu.CompilerParams(
            dimension_semantics=("arbitrary",), vmem_limit_bytes=VMEM_LIMIT),
        name="ffn_ple" if ple is not None else "ffn",
    )(*args)


PROJ_CHUNK = 256


def _proj_chunks(w_refs):
    out, dst = [], 0
    for w_ref in w_refs:
        width = w_ref.shape[-1]
        for lo in range(0, width, PROJ_CHUNK):
            hi = min(lo + PROJ_CHUNK, width)
            out.append((w_ref, lo, hi, dst + lo))
        dst += width
    return out


def _in_proj(x_ref, nw_ref, w_refs, proj_ref):
    u = _rms(x_ref[...], nw_ref[...]).astype(BF16)
    for w_ref, lo, hi, dst in _proj_chunks(w_refs):
        proj_ref[:, dst:dst + hi - lo] = _dot(u, w_ref[:, lo:hi])


class _Filler:
    def __init__(self, u_ref, w_refs, dst_ref):
        self.u_ref, self.dst_ref = u_ref, dst_ref
        self.todo = _proj_chunks(w_refs)

    def emit(self, n=1):
        for _ in range(n):
            if self.todo:
                w_ref, lo, hi, dst = self.todo.pop(0)
                self.dst_ref[:, dst:dst + hi - lo] = _dot(self.u_ref[...], w_ref[:, lo:hi])

    def drain(self):
        self.emit(len(self.todo))


def _ssd_chunk(c, dt_sc, xbc_sc, y_sc, h_sc, a_row, dsk, e3, tril, causal, lo_mask, filler):
    rows = slice(c * CHUNK, (c + 1) * CHUNK)
    dtc = dt_sc[rows, :]
    acum = _dot(tril, _split3(dtc * a_row))
    acum = acum[:, 0:LANES] + acum[:, LANES:2 * LANES] + acum[:, 2 * LANES:3 * LANES]
    a_t = acum.T[0:N_HEADS, :]
    dt_t = dtc.T[0:N_HEADS, :]
    w_t = dt_t * jnp.exp(a_t[:, CHUNK - 1:CHUNK] - a_t)
    cd = jnp.exp(_expand_heads(acum[CHUNK - 8:CHUNK, :], e3)[7:8, :])
    for g in range(N_GROUPS):
        b_g = xbc_sc[rows, D_SSD + g * D_STATE:D_SSD + (g + 1) * D_STATE]
        c_g = xbc_sc[rows, D_SSD + (N_GROUPS + g) * D_STATE:D_SSD + (N_GROUPS + g + 1) * D_STATE]
        bt_g = b_g.T
        s_g = _dot(c_g.astype(BF16), bt_g.astype(BF16))
        for k in range(HEADS_PER_GROUP // 2):
            lb = g * (HEADS_PER_GROUP // 2) + k
            cols = slice(lb * LANES, (lb + 1) * LANES)
            lhs, btw = [], []
            for r in (2 * lb, 2 * lb + 1):
                lq = acum[:, r:r + 1]
                dec = jnp.where(causal, jnp.exp(lq - a_t[r:r + 1, :]), 0.0)
                lhs.append((s_g * dec * dt_t[r:r + 1, :]).astype(BF16))
                lhs.append((c_g * jnp.exp(lq)).astype(BF16))
                btw.append((bt_g * w_t[r:r + 1, :]).astype(BF16))
            xs = xbc_sc[rows, cols]
            hp = h_sc[:, cols]
            x_lo = jnp.where(lo_mask, xs, 0.0).astype(BF16)
            x_hi = jnp.where(lo_mask, 0.0, xs).astype(BF16)
            h_lo = jnp.where(lo_mask, hp, 0.0).astype(BF16)
            h_hi = jnp.where(lo_mask, 0.0, hp).astype(BF16)
            y = _dot(jnp.concatenate(lhs, axis=1), jnp.concatenate([x_lo, h_lo, x_hi, h_hi], axis=0))
            y_sc[rows, cols] = y + dsk[:, cols] * xs
            upd = _dot(jnp.concatenate(btw, axis=1), jnp.concatenate([x_lo, x_hi], axis=0))
            h_sc[:, cols] = hp * cd[:, cols] + upd
            filler.emit()


def _pool_group_sums(ext, gi, w):
    s = ext[:, gi * POOL_GD:(gi + 1) * POOL_GD]
    sh = 1
    while sh < w:
        s = s + pltpu.roll(s, sh, 0)
        sh *= 2
    return s[16:, :]


def _mixp_kernel(x_ref, xn_ref, nw_ref, wzx_ref, wxp_ref, wdt_ref, cw_ref, cb_ref, dtb_ref, alog_ref, dsk_ref,
                 snw_ref, pw_ref, ps_ref, wout_ref, e3_ref,
                 xo_ref, ssm_ref, conv_ref, pool_ref,
                 h_sc, cext_sc, pext_sc, xbc_sc, dt_sc, y_sc, u_sc, proj_a, proj_b, *, L, n_tiles):
    i = pl.program_id(1)
    g = pl.program_id(0) * n_tiles + i
    w_refs = (wzx_ref, wxp_ref, wdt_ref)

    @pl.when(g == 0)
    def _():
        _in_proj(x_ref, nw_ref, w_refs, proj_a)

    @pl.when(i == 0)
    def _():
        h_sc[...] = jnp.zeros_like(h_sc)
        cext_sc[0:8, :] = jnp.zeros((8, CONV_DIM), F32)
        pext_sc[0:16, :] = jnp.zeros((16, POOL_W), F32)

    tile = functools.partial(
        _mixp_tile, i, x_ref, xn_ref, nw_ref, w_refs, cw_ref, cb_ref, dtb_ref, alog_ref, dsk_ref, snw_ref, pw_ref,
        ps_ref, wout_ref, e3_ref, xo_ref, h_sc, cext_sc, pext_sc, xbc_sc, dt_sc, y_sc, u_sc, L=L)

    @pl.when(g % 2 == 0)
    def _():
        tile(proj_a, proj_b)

    @pl.when(g % 2 == 1)
    def _():
        tile(proj_b, proj_a)

    @pl.when(i == n_tiles - 1)
    def _():
        ssm_ref[...] = h_sc[...].T
        conv_ref[...] = cext_sc[L + 5:L + 8, :]
        pool_ref[...] = pext_sc[L + 1:L + 16, :]

    cext_sc[0:8, :] = cext_sc[L:L + 8, :]
    pext_sc[0:16, :] = pext_sc[L:L + 16, :]


def _mixp_tile(i, x_ref, xn_ref, nw_ref, w_refs, cw_ref, cb_ref, dtb_ref, alog_ref, dsk_ref, snw_ref, pw_ref,
               ps_ref, wout_ref, e3_ref, xo_ref, h_sc, cext_sc, pext_sc, xbc_sc, dt_sc, y_sc, u_sc, proj_cur, proj_nxt,
               *, L):
    u_sc[...] = _rms(xn_ref[...], nw_ref[...]).astype(BF16)
    filler = _Filler(u_sc, w_refs, proj_nxt)

    cext_sc[8:8 + L, :] = proj_cur[:, O_XBC:O_XP]
    cw = cw_ref[...]
    cb = cb_ref[...]
    for c0 in range(0, CONV_DIM, PROJ_CHUNK):
        cs = slice(c0, c0 + PROJ_CHUNK)
        conv = cb[:, cs] + cw[3:4, cs] * cext_sc[8:8 + L, cs]
        for k in range(CONV_W - 1):
            conv = conv + cw[k:k + 1, cs] * cext_sc[5 + k:5 + k + L, cs]
        xbc_sc[:, cs] = _silu(conv)
        filler.emit()

    dt_sc[...] = _softplus(proj_cur[:, O_DT:O_END] + dtb_ref[...])
    a_row = -jnp.exp(alog_ref[...])
    filler.emit()

    ri = lax.broadcasted_iota(jnp.int32, (CHUNK, CHUNK), 0)
    ci = lax.broadcasted_iota(jnp.int32, (CHUNK, CHUNK), 1)
    causal = ri >= ci
    tril = causal.astype(BF16)
    lo_mask = ci < HEAD_DIM
    dsk = dsk_ref[...]
    e3 = e3_ref[...]
    xp = proj_cur[:, O_XP:O_DT]
    pext_sc[16:16 + L, :] = xp
    ext = pext_sc[...]
    t_abs = i * L + lax.broadcasted_iota(jnp.int32, (L, 1), 0)
    yps = []
    for gi, w in enumerate(POOL_WINDOWS):
        cnt = jnp.minimum(t_abs + 1, w).astype(F32)
        d = _pool_group_sums(ext, gi, w) / cnt - xp[:, gi * POOL_GD:(gi + 1) * POOL_GD]
        yps.append(_dot(d.astype(BF16), pw_ref[gi]))
    yp = (jnp.concatenate(yps, axis=1) * ps_ref[...]).astype(BF16)
    out_pool = _dot(yp, wout_ref[D_SSD:D_SSD + POOL_W, :])

    for c in range(L // CHUNK):
        _ssd_chunk(c, dt_sc, xbc_sc, y_sc, h_sc, a_row, dsk, e3, tril, causal, lo_mask, filler)
    filler.drain()

    yn = _rms(y_sc[...] * _silu(proj_cur[:, O_Z:O_XBC]), snw_ref[...]).astype(BF16)
    xo_ref[...] = x_ref[...] + (out_pool + _dot(yn, wout_ref[0:D_SSD, :]))


def _mixp_call(x, layer, w, depth, batch, seq, L, ssm_prev):
    n_tiles = seq // L
    d = x.shape[-1]
    last = batch * n_tiles - 1
    in_specs = [
        pl.BlockSpec((L, d), lambda b, i: (b * n_tiles + i, 0)),
        pl.BlockSpec((L, d), lambda b, i: (jnp.minimum(b * n_tiles + i + 1, last), 0)),
        _resident((1, d), layer),
        _resident((d, O_XP), layer),
        _resident((d, O_DT - O_XP), layer),
        _resident((d, O_END - O_DT), layer),
        _resident((CONV_W, CONV_DIM), layer),
        _resident((1, CONV_DIM), layer),
        _resident((1, LANES), layer),
        _resident((1, LANES), layer),
        _resident((1, D_SSD), layer),
        _resident((1, D_SSD), layer),
        _resident((len(POOL_WINDOWS), POOL_GD, POOL_GD), layer),
        _resident((1, POOL_W), layer),
        _resident((D_SSD + POOL_W, d), layer),
        _resident((3 * LANES, D_SSD)),
    ]
    args = [x, x, w["norm_mix"], w["w_zx"], w["w_xp"], w["w_dt"], w["conv_w"], w["conv_b"], w["dt_bias"],
            w["a_log"], w["d_skip"], w["ssd_norm_w"], w["pool_w"], w["pool_scale"], w["w_out"], w["e3"]]
    n_in = len(args)
    aliases = {}
    if ssm_prev is not None:
        in_specs.append(pl.BlockSpec(memory_space=pl.ANY))
        aliases = {n_in: 1}
        args.append(ssm_prev)
    out_specs = [
        pl.BlockSpec((L, d), lambda b, i: (b * n_tiles + i, 0)),
        pl.BlockSpec((None, None, D_SSD, D_STATE), lambda b, i: (layer, b, 0, 0)),
        pl.BlockSpec((None, CONV_W - 1, CONV_DIM), lambda b, i: (b, 0, 0)),
        pl.BlockSpec((None, POOL_BUF, POOL_W), lambda b, i: (b, 0, 0)),
    ]
    out_shape = [
        jax.ShapeDtypeStruct((batch * seq, d), F32),
        jax.ShapeDtypeStruct((depth, batch, D_SSD, D_STATE), F32),
        jax.ShapeDtypeStruct((batch, CONV_W - 1, CONV_DIM), F32),
        jax.ShapeDtypeStruct((batch, POOL_BUF, POOL_W), F32),
    ]
    scratch = [
        pltpu.VMEM((D_STATE, D_SSD), F32),
        pltpu.VMEM((8 + L, CONV_DIM), F32),
        pltpu.VMEM((16 + L, POOL_W), F32),
        pltpu.VMEM((L, CONV_DIM), F32),
        pltpu.VMEM((L, LANES), F32),
        pltpu.VMEM((L, D_SSD), F32),
        pltpu.VMEM((L, D_MODEL), BF16),
        pltpu.VMEM((L, O_END), F32),
        pltpu.VMEM((L, O_END), F32),
    ]

    def body(*refs):
        if ssm_prev is not None:
            refs = refs[:n_in] + refs[n_in + 1:]
        _mixp_kernel(*refs, L=L, n_tiles=n_tiles)

    return pl.pallas_call(
        body,
        grid=(batch, n_tiles),
        in_specs=in_specs,
        out_specs=out_specs,
        out_shape=out_shape,
        scratch_shapes=scratch,
        input_output_aliases=aliases,
        compiler_params=pltpu.CompilerParams(
            dimension_semantics=("arbitrary", "arbitrary"), vmem_limit_bytes=VMEM_LIMIT),
        name="mixer_prompt",
    )(*args)


def _store_seq8(ref, t, val):
    nb = val.shape[0]
    for k in range(ref.shape[0]):
        ref[k, pl.ds(t, nb, stride=8), :] = val[:, k * LANES:(k + 1) * LANES]


def _load_rows(ref, rows):
    return jnp.concatenate([ref[k, rows, :] for k in range(ref.shape[0])], axis=1)


def _sa_kernel(x_ref, sconv_ref, spool_ref, nw_ref, wzx_ref, wxp_ref, wdt_ref, cw_ref, cb_ref, dtb_ref, alog_ref,
               dsk_ref, pw_ref, ps_ref, e3_ref,
               z_ref, yp_ref, ypart_ref, ea_ref, xw_ref, cd3_ref, c_ref, b_ref, conv_ref, pool_ref,
               *, nb, nt):
    u = _rms(x_ref[...].reshape(nt * nb, -1), nw_ref[...]).astype(BF16)
    z_ref[...] = _dot(u, wzx_ref[:, O_Z:O_XBC]).reshape(nt, nb, D_SSD)
    xbc = _dot(u, wzx_ref[:, O_XBC:O_XP])
    xp = _dot(u, wxp_ref[...])
    dtr = _dot(u, wdt_ref[...])
    e3 = e3_ref[...]

    def tile(v, t):
        return v[t * nb:(t + 1) * nb, :]

    cext = [sconv_ref[k] for k in range(CONV_W - 1)]
    cext += [tile(xbc, t) for t in range(nt)]
    cw = cw_ref[...]
    xs, bm, cm = [], [], []
    for t in range(nt):
        acc = cb_ref[...]
        for k in range(CONV_W):
            acc = acc + cw[k:k + 1, :] * cext[t + k]
        v = _silu(acc)
        xs.append(v[:, 0:D_SSD])
        bm.append(v[:, D_SSD:D_SSD + N_GROUPS * D_STATE])
        cm.append(v[:, D_SSD + N_GROUPS * D_STATE:])
    for k in range(CONV_W - 1):
        conv_ref[:, k, :] = cext[nt + k]

    dt = _softplus(dtr + dtb_ref[...])
    a_row = -jnp.exp(alog_ref[...])
    dts = [tile(dt, t) for t in range(nt)]
    acum = []
    for t in range(nt):
        da = dts[t] * a_row
        acum.append(da if t == 0 else acum[-1] + da)
    a_last = acum[-1]

    for r in (xw_ref, cd3_ref, c_ref, b_ref):
        r[...] = jnp.zeros(r.shape, F32)
    lane = lax.broadcasted_iota(jnp.int32, (nb, LANES), 1)
    g0_heads = lane < HEADS_PER_GROUP
    dsk = dsk_ref[...]
    for t in range(nt):
        y = dsk * xs[t]
        for s in range(t + 1):
            sc = []
            for g in range(N_GROUPS):
                cg = cm[t][:, g * D_STATE:(g + 1) * D_STATE]
                bg = bm[s][:, g * D_STATE:(g + 1) * D_STATE]
                sc.append(jnp.sum(cg * bg, axis=1, keepdims=True))
            wts = jnp.where(g0_heads, sc[0], sc[1]) * jnp.exp(acum[t] - acum[s]) * dts[s]
            y = y + _expand_heads(wts, e3) * xs[s]
        ypart_ref[t] = y
        ea_ref[t] = _expand_heads(jnp.exp(acum[t]), e3)
        _store_seq8(xw_ref, t, _expand_heads(dts[t] * jnp.exp(a_last - acum[t]), e3) * xs[t])
        _store_seq8(c_ref, t, cm[t])
        _store_seq8(b_ref, t, bm[t])
    cd = jnp.exp(_expand_heads(a_last, e3))
    hi = cd.astype(BF16).astype(F32)
    mid = (cd - hi).astype(BF16).astype(F32)
    lo = (cd - hi - mid).astype(BF16).astype(F32)
    _store_seq8(cd3_ref, 0, hi)
    _store_seq8(cd3_ref, 1, mid)
    _store_seq8(cd3_ref, 2, lo)

    pext = [spool_ref[k] for k in range(POOL_BUF)]
    pext += [tile(xp, t) for t in range(nt)]
    for t in range(nt):
        yps = []
        for gi, w in enumerate(POOL_WINDOWS):
            cols = slice(gi * POOL_GD, (gi + 1) * POOL_GD)
            s = pext[POOL_BUF + t][:, cols]
            for k in range(1, w):
                s = s + pext[POOL_BUF + t - k][:, cols]
            d = s / float(w) - pext[POOL_BUF + t][:, cols]
            yps.append(_dot(d.astype(BF16), pw_ref[gi]))
        yp_ref[t] = (jnp.concatenate(yps, axis=1) * ps_ref[...]).astype(BF16)
    for k in range(POOL_BUF):
        pool_ref[:, k, :] = pext[nt + k]


def _sa_call(x_dec, sconv, spool, layer, w, depth, sb, conv_prev, pool_prev):
    nt, nb, d = x_dec.shape
    in_specs = [
        pl.BlockSpec((nt, sb, d), lambda i: (0, i, 0)),
        pl.BlockSpec((None, CONV_W - 1, sb, CONV_DIM), lambda i: (layer, 0, i, 0)),
        pl.BlockSpec((None, POOL_BUF, sb, POOL_W), lambda i: (layer, 0, i, 0)),
        _resident((1, d), layer),
        _resident((d, O_XP), layer),
        _resident((d, O_DT - O_XP), layer),
        _resident((d, O_END - O_DT), layer),
        _resident((CONV_W, CONV_DIM), layer),
        _resident((1, CONV_DIM), layer),
        _resident((1, LANES), layer),
        _resident((1, LANES), layer),
        _resident((1, D_SSD), layer),
        _resident((len(POOL_WINDOWS), POOL_GD, POOL_GD), layer),
        _resident((1, POOL_W), layer),
        _resident((3 * LANES, D_SSD)),
    ]
    args = [x_dec, sconv, spool, w["norm_mix"], w["w_zx"], w["w_xp"], w["w_dt"], w["conv_w"], w["conv_b"],
            w["dt_bias"], w["a_log"], w["d_skip"], w["pool_w"], w["pool_scale"], w["e3"]]
    n_in = len(args)
    aliases = {}
    if conv_prev is not None:
        in_specs += [pl.BlockSpec(memory_space=pl.ANY), pl.BlockSpec(memory_space=pl.ANY)]
        aliases = {n_in: 8, n_in + 1: 9}
        args += [conv_prev, pool_prev]
    gn = N_GROUPS * D_STATE
    tm_outs = [(D_SSD, F32), (POOL_W, BF16), (D_SSD, F32), (D_SSD, F32)]
    out_specs = [pl.BlockSpec((nt, sb, wd), lambda i: (0, i, 0)) for wd, _ in tm_outs]
    out_shape = [jax.ShapeDtypeStruct((nt, nb, wd), dt) for wd, dt in tm_outs]
    for wd in (D_SSD, D_SSD, gn, gn):
        out_specs.append(pl.BlockSpec((wd // LANES, sb * 8, LANES), lambda i: (0, i, 0)))
        out_shape.append(jax.ShapeDtypeStruct((wd // LANES, nb * 8, LANES), F32))
    for n_rows, wd in ((CONV_W - 1, CONV_DIM), (POOL_BUF, POOL_W)):
        out_specs.append(pl.BlockSpec((None, sb, n_rows, wd), lambda i: (layer, i, 0, 0)))
        out_shape.append(jax.ShapeDtypeStruct((depth, nb, n_rows, wd), F32))

    def body(*refs):
        if conv_prev is not None:
            refs = refs[:n_in] + refs[n_in + 2:]
        _sa_kernel(*refs, nb=sb, nt=nt)

    return pl.pallas_call(
        body,
        grid=(nb // sb,),
        in_specs=in_specs,
        out_specs=out_specs,
        out_shape=out_shape,
        input_output_aliases=aliases,
        compiler_params=pltpu.CompilerParams(
            dimension_semantics=("arbitrary",), vmem_limit_bytes=VMEM_LIMIT),
        name="mixer_decode_tokens",
    )(*args)


def _sb_kernel(h_ref, c_ref, b_ref, xw_ref, cd3_ref, hn_ref, ch_ref, *, sblk):
    half = D_SSD // N_GROUPS
    ones = jnp.ones((8, D_STATE), BF16)
    for j in range(sblk):
        rows = slice(j * 8, (j + 1) * 8)
        h0 = h_ref[j]
        hb = h0.astype(BF16)
        cj = _load_rows(c_ref, rows).astype(BF16)
        bj = _load_rows(b_ref, rows).astype(BF16)
        xw = _load_rows(xw_ref, rows).astype(BF16)
        dmat = _dot_t0(_load_rows(cd3_ref, rows).astype(BF16), ones)
        chs, upds = [], []
        for g in range(N_GROUPS):
            rs = slice(g * half, (g + 1) * half)
            ns = slice(g * D_STATE, (g + 1) * D_STATE)
            chs.append(_dot_t1(cj[:, ns], hb[rs, :]))
            upds.append(_dot_t0(xw[:, rs], bj[:, ns]))
        ch = jnp.concatenate(chs, axis=1)
        for k in range(ch_ref.shape[0]):
            ch_ref[k, rows, :] = ch[:, k * LANES:(k + 1) * LANES]
        hn_ref[j] = h0 * dmat + jnp.concatenate(upds, axis=0)


def _sb_call(h0, c8, b8, xw8, cd38, layer, depth, nb, sblk, ssm_prev):
    gn = N_GROUPS * D_STATE
    in_specs = [
        pl.BlockSpec((None, sblk, D_SSD, D_STATE), lambda i: (layer, i, 0, 0)),
        pl.BlockSpec((gn // LANES, sblk * 8, LANES), lambda i: (0, i, 0)),
        pl.BlockSpec((gn // LANES, sblk * 8, LANES), lambda i: (0, i, 0)),
        pl.BlockSpec((D_SSD // LANES, sblk * 8, LANES), lambda i: (0, i, 0)),
        pl.BlockSpec((D_SSD // LANES, sblk * 8, LANES), lambda i: (0, i, 0)),
    ]
    args = [h0, c8, b8, xw8, cd38]
    n_in = len(args)
    aliases = {}
    if ssm_prev is not None:
        in_specs.append(pl.BlockSpec(memory_space=pl.ANY))
        aliases = {n_in: 0}
        args.append(ssm_prev)

    def body(*refs):
        if ssm_prev is not None:
            refs = refs[:n_in] + refs[n_in + 1:]
        _sb_kernel(*refs, sblk=sblk)

    return pl.pallas_call(
        body,
        grid=(nb // sblk,),
        in_specs=in_specs,
        out_specs=[
            pl.BlockSpec((None, sblk, D_SSD, D_STATE), lambda i: (layer, i, 0, 0)),
            pl.BlockSpec((D_SSD // LANES, sblk * 8, LANES), lambda i: (0, i, 0)),
        ],
        out_shape=[
            jax.ShapeDtypeStruct((depth, nb, D_SSD, D_STATE), F32),
            jax.ShapeDtypeStruct((D_SSD // LANES, nb * 8, LANES), F32),
        ],
        input_output_aliases=aliases,
        compiler_params=pltpu.CompilerParams(
            dimension_semantics=("parallel",), vmem_limit_bytes=VMEM_LIMIT),
        name="mixer_decode_state",
    )(*args)


def _sc_kernel(x_ref, z_ref, ypart_ref, ea_ref, ch_ref, yp_ref, snw_ref, wout_ref, o_ref, *, nb, nt):
    ch =jnp.concatenate([_load_rows(ch_ref, pl.ds(t, nb, stride=8)) for t in range(nt)], axis=0)
    y = ypart_ref[...] + ea_ref[...] * ch
    yn = _rms(y * _silu(z_ref[...]), snw_ref[...]).astype(BF16)
    out = _dot(jnp.concatenate([yn, yp_ref[...]], axis=1), wout_ref[...])
    o_ref[...] = x_ref[...] + out


def _sc_call(x, z, ypart, ea, ch8, yp, layer, w, nb, nt):
    t_all, d = x.shape
    rows = nt * nb
    xblk = (t_all - rows) // rows

    def full(shape):
        return pl.BlockSpec(shape, lambda i: (0, 0))

    return pl.pallas_call(
        functools.partial(_sc_kernel, nb=nb, nt=nt),
        grid=(1,),
        in_specs=[pl.BlockSpec((rows, d), lambda i: (xblk, 0)), full((rows, D_SSD)), full((rows, D_SSD)),
                  full((rows, D_SSD)), pl.BlockSpec((D_SSD // LANES, nb * 8, LANES), lambda i: (0, 0, 0)),
                  full((rows, POOL_W)),
                  _resident((1, D_SSD), layer), _resident((D_SSD + POOL_W, d), layer)],
        out_specs=pl.BlockSpec((rows, d), lambda i: (0, 0)),
        out_shape=jax.ShapeDtypeStruct((rows, d), F32),
        compiler_params=pltpu.CompilerParams(vmem_limit_bytes=VMEM_LIMIT),
        name="mixer_decode_out",
    )(x, z, ypart, ea, ch8, yp, w["ssd_norm_w"], w["w_out"])


def _prep_weights(w_in, conv_w, conv_b, dt_bias, a_log, d_skip, ssd_norm_w, pool_w, pool_scale, w_out,
                  norm_mix):
    depth = w_in.shape[0]
    o1, o2, o3 = D_SSD, D_SSD + CONV_DIM, D_SSD + CONV_DIM + N_HEADS
    w_dt = jnp.pad(w_in[:, :, o2:o3], ((0, 0), (0, 0), (0, LANES - N_HEADS)))
    pad_h = ((0, 0), (0, LANES - N_HEADS))
    head_of_lane = jnp.arange(D_SSD) // HEAD_DIM
    e1 = (jnp.arange(LANES)[:, None] == head_of_lane[None, :]).astype(BF16)
    return {
        "norm_mix": norm_mix.reshape(depth, 1, -1),
        "w_zx": w_in[:, :, :o2].astype(BF16),
        "w_xp": w_in[:, :, o3:].astype(BF16),
        "w_dt": w_dt.astype(BF16),
        "conv_w": conv_w,
        "conv_b": conv_b.reshape(depth, 1, -1),
        "dt_bias": jnp.pad(dt_bias, pad_h).reshape(depth, 1, LANES),
        "a_log": jnp.pad(a_log, pad_h).reshape(depth, 1, LANES),
        "d_skip": jnp.repeat(d_skip, HEAD_DIM, axis=-1).reshape(depth, 1, D_SSD),
        "ssd_norm_w": ssd_norm_w.reshape(depth, 1, -1),
        "pool_w": pool_w.astype(BF16),
        "pool_scale": pool_scale.reshape(depth, 1, -1),
        "w_out": w_out.astype(BF16),
        "e3": jnp.concatenate([e1, e1, e1], axis=0),
    }


def kernel(x_prompt, x_sample, p_prompt, p_sample, state_ssm, state_conv, state_pool, w_in, conv_w, conv_b,
           dt_bias, a_log, d_skip, ssd_norm_w, pool_w, pool_scale, w_out, norm_ffn1, ffn1_gate, ffn1_up,
           ffn1_down, norm_mix, norm_ffn2, ffn2_gate, ffn2_up, ffn2_down, norm_ple, ple_gate, ple_proj,
           final_norm):
    L, sblk = MIX_TILE, DEC_SEQ_BLOCK
    depth = w_in.shape[0]
    batch, seq, d = x_prompt.shape
    nb, nt, _ = x_sample.shape
    n_prompt, n_dec = batch * seq, nt * nb
    tm = n_dec
    n_main = n_prompt // tm
    assert n_main * tm == n_prompt
    mw = _prep_weights(w_in, conv_w, conv_b, dt_bias, a_log, d_skip, ssd_norm_w, pool_w, pool_scale, w_out,
                       norm_mix)
    f1 = (norm_ffn1.reshape(depth, 1, d), ffn1_gate.astype(BF16), ffn1_up.astype(BF16), ffn1_down.astype(BF16))
    f2 = (norm_ffn2.reshape(depth, 1, d), ffn2_gate.astype(BF16), ffn2_up.astype(BF16), ffn2_down.astype(BF16))
    ple = (p_prompt.reshape(depth, n_prompt, -1), jnp.transpose(p_sample, (0, 2, 1, 3)).reshape(depth, n_dec, -1),
           norm_ple.reshape(depth, 1, d), ple_gate.astype(BF16), ple_proj.astype(BF16), final_norm.reshape(1, d))
    x = (x_prompt.reshape(n_prompt, d), jnp.transpose(x_sample, (1, 0, 2)).reshape(n_dec, d))
    h_all = state_ssm.reshape(depth, nb, D_SSD, D_STATE)
    sconv_tm = jnp.transpose(state_conv, (0, 2, 1, 3))
    spool_tm = jnp.transpose(state_pool, (0, 2, 1, 3))

    ssm_p = ssm_s = conv_s = pool_s = None
    conv_p, pool_p = [], []
    for i in range(depth):
        last = i == depth - 1
        x = _ffn_call(x, i, *f1, tm, n_main)
        xm, ssm_p, cp, plp = _mixp_call(x, i, mw, depth, batch, seq, L, ssm_p)
        z, yp, ypart, ea, xw8, cd38, c8, b8, conv_s, pool_s = _sa_call(
            x[n_prompt:].reshape(nt, nb, d), sconv_tm, spool_tm, i, mw, depth, DEC_TOKEN_SEQ_BLOCK, conv_s, pool_s)
        z, yp, ypart, ea = (v.reshape(n_dec, -1) for v in (z, yp, ypart, ea))
        ssm_s, ch8 = _sb_call(h_all, c8, b8, xw8, cd38, i, depth, nb, sblk, ssm_s)
        xd = _sc_call(x, z, ypart, ea, ch8, yp, i, mw, nb, nt)
        x = _ffn_call((xm, xd), i, *f2, tm, n_main, ple=ple, final=last, split_out=last)
        conv_p.append(cp)
        pool_p.append(plp)
    y_prompt = x[0].reshape(batch, seq, d)
    y_sample = jnp.transpose(x[1].reshape(nt, nb, d), (1, 0, 2))
    return (y_prompt, y_sample,
            ssm_p.reshape(depth, batch, N_HEADS, HEAD_DIM, D_STATE), jnp.stack(conv_p), jnp.stack(pool_p),
            ssm_s.reshape(depth, nb, N_HEADS, HEAD_DIM, D_STATE),
            conv_s, pool_s)
```

```python
import functools

import jax
import jax.numpy as jnp
from jax import lax
from jax.experimental import pallas as pl
from jax.experimental.pallas import tpu as pltpu

F32 = jnp.float32
BF16 = jnp.bfloat16

EPS = 1e-6
D_MODEL = 1024
D_SSD = 1024
HEAD_DIM = 64
N_HEADS = 16
N_GROUPS = 2
HEADS_PER_GROUP = 8
D_STATE = 128
CONV_W = 4
CONV_DIM = D_SSD + 2 * N_GROUPS * D_STATE
CHUNK = 128
POOL_WINDOWS = (2, 4, 8, 16)
POOL_GD = 256
POOL_BUF = 15
POOL_W = 1024
LANES = 128
O_Z, O_XBC, O_XP, O_DT, O_END = 0, 1024, 2560, 3584, 3712
VMEM_LIMIT = 56 * 1024 * 1024
MIX_TILE = 256
DEC_SEQ_BLOCK = 8
DEC_TOKEN_SEQ_BLOCK = 64


def _dot(a, b):
    return jnp.dot(a, b, preferred_element_type=F32)


def _dot_t0(a, b):
    return lax.dot_general(a, b, (((0,), (0,)), ((), ())), preferred_element_type=F32)


def _dot_t1(a, b):
    return lax.dot_general(a, b, (((1,), (1,)), ((), ())), preferred_element_type=F32)


def _rms(x, w):
    return x * lax.rsqrt(jnp.mean(x * x, axis=-1, keepdims=True) + EPS) * w


def _silu(x):
    return x * jax.nn.sigmoid(x)


def _softplus(x):
    return jnp.maximum(x, 0.0) + jnp.log1p(jnp.exp(-jnp.abs(x)))


def _split3(v):
    hi = v.astype(BF16)
    r1 = v - hi.astype(F32)
    mid = r1.astype(BF16)
    lo = (r1 - mid.astype(F32)).astype(BF16)
    return jnp.concatenate([hi, mid, lo], axis=1)


def _expand_heads(v, e3):
    return _dot(_split3(v), e3)


FF_CHUNK = 512


def _ffn_kernel(*refs, layer, ple, final, n_main, split_in, split_out):
    refs = list(refs)
    step = pl.program_id(0)
    is_main = step < n_main
    if split_in:
        xa_ref, xb_ref = refs[:2]
        x = jnp.where(is_main, xa_ref[...], xb_ref[...])
        refs = refs[2:]
    else:
        x = refs[0][...]
        refs = refs[1:]
    nw_ref, wg_hbm, wu_hbm, wd_hbm = refs[:4]
    refs = refs[4:]
    if ple:
        pa_ref, pb_ref, npw_ref, pg_ref, pp_ref, fn_ref = refs[:6]
        refs = refs[6:]
    wg_sc, wu_sc, wd_sc, stage_g, stage_u, stage_d, sem, acc_sc = refs[-8:]
    refs = refs[:-8]
    ff = wg_sc.shape[-1]
    chunks = [(c0, min(c0 + FF_CHUNK, ff)) for c0 in range(0, ff, FF_CHUNK)]

    def copies(c, slot):
        c0, c1 = chunks[c]
        w = c1 - c0
        return (pltpu.make_async_copy(wg_hbm.at[layer, :, c0:c1], stage_g.at[slot, :, 0:w], sem.at[0, slot]),
                pltpu.make_async_copy(wu_hbm.at[layer, :, c0:c1], stage_u.at[slot, :, 0:w], sem.at[1, slot]),
                pltpu.make_async_copy(wd_hbm.at[layer, c0:c1, :], stage_d.at[slot, 0:w, :], sem.at[2, slot]))

    u = _rms(x, nw_ref[...]).astype(BF16)

    def chunk_product(c):
        c0, c1 = chunks[c]
        g = _dot(u, wg_sc[:, c0:c1])
        up = _dot(u, wu_sc[:, c0:c1])
        return _dot((_silu(g) * up).astype(BF16), wd_sc[c0:c1, :])

    @pl.when(step == 0)
    def _():
        for cp in copies(0, 0):
            cp.start()
        acc = None
        for c, (c0, c1) in enumerate(chunks):
            slot = c % 2
            if c + 1 < len(chunks):
                for cp in copies(c + 1, 1 - slot):
                    cp.start()
            for cp in copies(c, slot):
                cp.wait()
            w = c1 - c0
            wg_sc[:, c0:c1] = stage_g[slot, :, 0:w].astype(BF16)
            wu_sc[:, c0:c1] = stage_u[slot, :, 0:w].astype(BF16)
            wd_sc[c0:c1, :] = stage_d[slot, 0:w, :].astype(BF16)
            d = chunk_product(c)
            acc = d if acc is None else acc + d
        acc_sc[...] = acc

    @pl.when(step > 0)
    def _():
        acc = None
        for c in range(len(chunks)):
            d = chunk_product(c)
            acc = d if acc is None else acc + d
        acc_sc[...] = acc

    x1 = x + 0.5 * acc_sc[...]
    if ple:
        un = _rms(x1, npw_ref[...]).astype(BF16)
        gate = jax.nn.sigmoid(_dot(un, pg_ref[...]))
        p = jnp.where(is_main, pa_ref[...], pb_ref[...]).astype(BF16)
        x1 = x1 + gate * _dot(p, pp_ref[...])
        if final:
            x1 = _rms(x1, fn_ref[...])
    if split_out:
        oa_ref, ob_ref = refs

        @pl.when(is_main)
        def _():
            oa_ref[...] = x1

        @pl.when(jnp.logical_not(is_main))
        def _():
            ob_ref[...] = x1
    else:
        refs[0][...] = x1


def _resident(shape, layer=None):
    n = len(shape)
    if layer is None:
        return pl.BlockSpec(shape, lambda *_: (0,) * n, pipeline_mode=pl.Buffered(1))
    return pl.BlockSpec((None,) + shape, lambda *_: (layer,) + (0,) * n, pipeline_mode=pl.Buffered(1))


def _ffn_call(x, layer, nw, wg, wu, wd, tm, n_main, ple=None, final=False, split_out=False):
    split_in = isinstance(x, tuple)
    d = wg.shape[1]
    ff = wg.shape[-1]

    def main_blk(i):
        return (jnp.minimum(i, n_main - 1), 0)

    def tail_blk(i):
        return (0, 0)

    if split_in:
        assert x[0].shape == (n_main * tm, d) and x[1].shape == (tm, d)
        in_specs = [pl.BlockSpec((tm, d), main_blk), pl.BlockSpec((tm, d), tail_blk)]
        args = list(x)
    else:
        assert x.shape == ((n_main + 1) * tm, d)
        in_specs = [pl.BlockSpec((tm, d), lambda i: (i, 0))]
        args = [x]
    in_specs += [_resident((1, d), layer)] + [pl.BlockSpec(memory_space=pl.ANY)] * 3
    args += [nw, wg, wu, wd]
    if ple is not None:
        pa, pb, npw, pg, pp, fn = ple
        pd = pa.shape[-1]
        in_specs += [
            pl.BlockSpec((None, tm, pd), lambda i: (layer, jnp.minimum(i, n_main - 1), 0)),
            pl.BlockSpec((None, tm, pd), lambda i: (layer, 0, 0)),
            _resident((1, d), layer),
            _resident((d, d), layer),
            _resident((pd, d), layer),
            _resident((1, d)),
        ]
        args += [pa, pb, npw, pg, pp, fn]
    if split_out:
        out_specs = [pl.BlockSpec((tm, d), main_blk), pl.BlockSpec((tm, d), tail_blk)]
        out_shape = [jax.ShapeDtypeStruct((n_main * tm, d), F32), jax.ShapeDtypeStruct((tm, d), F32)]
    else:
        out_specs = pl.BlockSpec((tm, d), lambda i: (i, 0))
        out_shape = jax.ShapeDtypeStruct(((n_main + 1) * tm, d), F32)
    return pl.pallas_call(
        functools.partial(_ffn_kernel, layer=layer, ple=ple is not None, final=final, n_main=n_main,
                          split_in=split_in, split_out=split_out),
        grid=(n_main + 1,),
        in_specs=in_specs,
        out_specs=out_specs,
        out_shape=out_shape,
        scratch_shapes=[
            pltpu.VMEM((d, ff), BF16), pltpu.VMEM((d, ff), BF16), pltpu.VMEM((ff, d), BF16),
            pltpu.VMEM((2, d, FF_CHUNK), F32), pltpu.VMEM((2, d, FF_CHUNK), F32), pltpu.VMEM((2, FF_CHUNK, d), F32),
            pltpu.SemaphoreType.DMA((3, 2)), pltpu.VMEM((tm, d), F32)],
        compiler_params=pltpu.CompilerParams(
            dimension_semantics=("arbitrary",), vmem_limit_bytes=VMEM_LIMIT),
        name="ffn_ple" if ple is not None else "ffn",
    )(*args)


PROJ_CHUNK = 256


def _proj_chunks(w_refs):
    out, dst = [], 0
    for w_ref in w_refs:
        width = w_ref.shape[-1]
        for lo in range(0, width, PROJ_CHUNK):
            hi = min(lo + PROJ_CHUNK, width)
            out.append((w_ref, lo, hi, dst + lo))
        dst += width
    return out


def _in_proj(x_ref, nw_ref, w_refs, proj_ref):
    u = _rms(x_ref[...], nw_ref[...]).astype(BF16)
    for w_ref, lo, hi, dst in _proj_chunks(w_refs):
        proj_ref[:, dst:dst + hi - lo] = _dot(u, w_ref[:, lo:hi])


class _Filler:
    def __init__(self, u_ref, w_refs, dst_ref):
        self.u_ref, self.dst_ref = u_ref, dst_ref
        self.todo = _proj_chunks(w_refs)

    def emit(self, n=1):
        for _ in range(n):
            if self.todo:
                w_ref, lo, hi, dst = self.todo.pop(0)
                self.dst_ref[:, dst:dst + hi - lo] = _dot(self.u_ref[...], w_ref[:, lo:hi])

    def drain(self):
        self.emit(len(self.todo))


def _ssd_chunk(c, dt_sc, xbc_sc, y_sc, h_sc, a_row, dsk, e3, tril, causal, lo_mask, filler):
    rows = slice(c * CHUNK, (c + 1) * CHUNK)
    dtc = dt_sc[rows, :]
    acum = _dot(tril, _split3(dtc * a_row))
    acum = acum[:, 0:LANES] + acum[:, LANES:2 * LANES] + acum[:, 2 * LANES:3 * LANES]
    a_t = acum.T[0:N_HEADS, :]
    dt_t = dtc.T[0:N_HEADS, :]
    w_t = dt_t * jnp.exp(a_t[:, CHUNK - 1:CHUNK] - a_t)
    cd = jnp.exp(_expand_heads(acum[CHUNK - 8:CHUNK, :], e3)[7:8, :])
    for g in range(N_GROUPS):
        b_g = xbc_sc[rows, D_SSD + g * D_STATE:D_SSD + (g + 1) * D_STATE]
        c_g = xbc_sc[rows, D_SSD + (N_GROUPS + g) * D_STATE:D_SSD + (N_GROUPS + g + 1) * D_STATE]
        bt_g = b_g.T
        s_g = _dot(c_g.astype(BF16), bt_g.astype(BF16))
        for k in range(HEADS_PER_GROUP // 2):
            lb = g * (HEADS_PER_GROUP // 2) + k
            cols = slice(lb * LANES, (lb + 1) * LANES)
            lhs, btw = [], []
            for r in (2 * lb, 2 * lb + 1):
                lq = acum[:, r:r + 1]
                dec = jnp.where(causal, jnp.exp(lq - a_t[r:r + 1, :]), 0.0)
                lhs.append((s_g * dec * dt_t[r:r + 1, :]).astype(BF16))
                lhs.append((c_g * jnp.exp(lq)).astype(BF16))
                btw.append((bt_g * w_t[r:r + 1, :]).astype(BF16))
            xs = xbc_sc[rows, cols]
            hp = h_sc[:, cols]
            x_lo = jnp.where(lo_mask, xs, 0.0).astype(BF16)
            x_hi = jnp.where(lo_mask, 0.0, xs).astype(BF16)
            h_lo = jnp.where(lo_mask, hp, 0.0).astype(BF16)
            h_hi = jnp.where(lo_mask, 0.0, hp).astype(BF16)
            y = _dot(jnp.concatenate(lhs, axis=1), jnp.concatenate([x_lo, h_lo, x_hi, h_hi], axis=0))
            y_sc[rows, cols] = y + dsk[:, cols] * xs
            upd = _dot(jnp.concatenate(btw, axis=1), jnp.concatenate([x_lo, x_hi], axis=0))
            h_sc[:, cols] = hp * cd[:, cols] + upd
            filler.emit()


def _pool_group_sums(ext, gi, w):
    s = ext[:, gi * POOL_GD:(gi + 1) * POOL_GD]
    sh = 1
    while sh < w:
        s = s + pltpu.roll(s, sh, 0)
        sh *= 2
    return s[16:, :]


def _mixp_kernel(x_ref, xn_ref, nw_ref, wzx_ref, wxp_ref, wdt_ref, cw_ref, cb_ref, dtb_ref, alog_ref, dsk_ref,
                 snw_ref, pw_ref, ps_ref, wout_ref, e3_ref,
                 xo_ref, ssm_ref, conv_ref, pool_ref,
                 h_sc, cext_sc, pext_sc, xbc_sc, dt_sc, y_sc, u_sc, proj_a, proj_b, *, L, n_tiles):
    i = pl.program_id(1)
    g = pl.program_id(0) * n_tiles + i
    w_refs = (wzx_ref, wxp_ref, wdt_ref)

    @pl.when(g == 0)
    def _():
        _in_proj(x_ref, nw_ref, w_refs, proj_a)

    @pl.when(i == 0)
    def _():
        h_sc[...] = jnp.zeros_like(h_sc)
        cext_sc[0:8, :] = jnp.zeros((8, CONV_DIM), F32)
        pext_sc[0:16, :] = jnp.zeros((16, POOL_W), F32)

    tile = functools.partial(
        _mixp_tile, i, x_ref, xn_ref, nw_ref, w_refs, cw_ref, cb_ref, dtb_ref, alog_ref, dsk_ref, snw_ref, pw_ref,
        ps_ref, wout_ref, e3_ref, xo_ref, h_sc, cext_sc, pext_sc, xbc_sc, dt_sc, y_sc, u_sc, L=L)

    @pl.when(g % 2 == 0)
    def _():
        tile(proj_a, proj_b)

    @pl.when(g % 2 == 1)
    def _():
        tile(proj_b, proj_a)

    @pl.when(i == n_tiles - 1)
    def _():
        ssm_ref[...] = h_sc[...].T
        conv_ref[...] = cext_sc[L + 5:L + 8, :]
        pool_ref[...] = pext_sc[L + 1:L + 16, :]

    cext_sc[0:8, :] = cext_sc[L:L + 8, :]
    pext_sc[0:16, :] = pext_sc[L:L + 16, :]


def _mixp_tile(i, x_ref, xn_ref, nw_ref, w_refs, cw_ref, cb_ref, dtb_ref, alog_ref, dsk_ref, snw_ref, pw_ref,
               ps_ref, wout_ref, e3_ref, xo_ref, h_sc, cext_sc, pext_sc, xbc_sc, dt_sc, y_sc, u_sc, proj_cur, proj_nxt,
               *, L):
    u_sc[...] = _rms(xn_ref[...], nw_ref[...]).astype(BF16)
    filler = _Filler(u_sc, w_refs, proj_nxt)

    cext_sc[8:8 + L, :] = proj_cur[:, O_XBC:O_XP]
    cw = cw_ref[...]
    cb = cb_ref[...]
    for c0 in range(0, CONV_DIM, PROJ_CHUNK):
        cs = slice(c0, c0 + PROJ_CHUNK)
        conv = cb[:, cs] + cw[3:4, cs] * cext_sc[8:8 + L, cs]
        for k in range(CONV_W - 1):
            conv = conv + cw[k:k + 1, cs] * cext_sc[5 + k:5 + k + L, cs]
        xbc_sc[:, cs] = _silu(conv)
        filler.emit()

    dt_sc[...] = _softplus(proj_cur[:, O_DT:O_END] + dtb_ref[...])
    a_row = -jnp.exp(alog_ref[...])
    filler.emit()

    ri = lax.broadcasted_iota(jnp.int32, (CHUNK, CHUNK), 0)
    ci = lax.broadcasted_iota(jnp.int32, (CHUNK, CHUNK), 1)
    causal = ri >= ci
    tril = causal.astype(BF16)
    lo_mask = ci < HEAD_DIM
    dsk = dsk_ref[...]
    e3 = e3_ref[...]
    xp = proj_cur[:, O_XP:O_DT]
    pext_sc[16:16 + L, :] = xp
    ext = pext_sc[...]
    t_abs = i * L + lax.broadcasted_iota(jnp.int32, (L, 1), 0)
    yps = []
    for gi, w in enumerate(POOL_WINDOWS):
        cnt = jnp.minimum(t_abs + 1, w).astype(F32)
        d = _pool_group_sums(ext, gi, w) / cnt - xp[:, gi * POOL_GD:(gi + 1) * POOL_GD]
        yps.append(_dot(d.astype(BF16), pw_ref[gi]))
    yp = (jnp.concatenate(yps, axis=1) * ps_ref[...]).astype(BF16)
    out_pool = _dot(yp, wout_ref[D_SSD:D_SSD + POOL_W, :])

    for c in range(L // CHUNK):
        _ssd_chunk(c, dt_sc, xbc_sc, y_sc, h_sc, a_row, dsk, e3, tril, causal, lo_mask, filler)
    filler.drain()

    yn = _rms(y_sc[...] * _silu(proj_cur[:, O_Z:O_XBC]), snw_ref[...]).astype(BF16)
    xo_ref[...] = x_ref[...] + (out_pool + _dot(yn, wout_ref[0:D_SSD, :]))


def _mixp_call(x, layer, w, depth, batch, seq, L, ssm_prev):
    n_tiles = seq // L
    d = x.shape[-1]
    last = batch * n_tiles - 1
    in_specs = [
        pl.BlockSpec((L, d), lambda b, i: (b * n_tiles + i, 0)),
        pl.BlockSpec((L, d), lambda b, i: (jnp.minimum(b * n_tiles + i + 1, last), 0)),
        _resident((1, d), layer),
        _resident((d, O_XP), layer),
        _resident((d, O_DT - O_XP), layer),
        _resident((d, O_END - O_DT), layer),
        _resident((CONV_W, CONV_DIM), layer),
        _resident((1, CONV_DIM), layer),
        _resident((1, LANES), layer),
        _resident((1, LANES), layer),
        _resident((1, D_SSD), layer),
        _resident((1, D_SSD), layer),
        _resident((len(POOL_WINDOWS), POOL_GD, POOL_GD), layer),
        _resident((1, POOL_W), layer),
        _resident((D_SSD + POOL_W, d), layer),
        _resident((3 * LANES, D_SSD)),
    ]
    args = [x, x, w["norm_mix"], w["w_zx"], w["w_xp"], w["w_dt"], w["conv_w"], w["conv_b"], w["dt_bias"],
            w["a_log"], w["d_skip"], w["ssd_norm_w"], w["pool_w"], w["pool_scale"], w["w_out"], w["e3"]]
    n_in = len(args)
    aliases = {}
    if ssm_prev is not None:
        in_specs.append(pl.BlockSpec(memory_space=pl.ANY))
        aliases = {n_in: 1}
        args.append(ssm_prev)
    out_specs = [
        pl.BlockSpec((L, d), lambda b, i: (b * n_tiles + i, 0)),
        pl.BlockSpec((None, None, D_SSD, D_STATE), lambda b, i: (layer, b, 0, 0)),
        pl.BlockSpec((None, CONV_W - 1, CONV_DIM), lambda b, i: (b, 0, 0)),
        pl.BlockSpec((None, POOL_BUF, POOL_W), lambda b, i: (b, 0, 0)),
    ]
    out_shape = [
        jax.ShapeDtypeStruct((batch * seq, d), F32),
        jax.ShapeDtypeStruct((depth, batch, D_SSD, D_STATE), F32),
        jax.ShapeDtypeStruct((batch, CONV_W - 1, CONV_DIM), F32),
        jax.ShapeDtypeStruct((batch, POOL_BUF, POOL_W), F32),
    ]
    scratch = [
        pltpu.VMEM((D_STATE, D_SSD), F32),
        pltpu.VMEM((8 + L, CONV_DIM), F32),
        pltpu.VMEM((16 + L, POOL_W), F32),
        pltpu.VMEM((L, CONV_DIM), F32),
        pltpu.VMEM((L, LANES), F32),
        pltpu.VMEM((L, D_SSD), F32),
        pltpu.VMEM((L, D_MODEL), BF16),
        pltpu.VMEM((L, O_END), F32),
        pltpu.VMEM((L, O_END), F32),
    ]

    def body(*refs):
        if ssm_prev is not None:
            refs = refs[:n_in] + refs[n_in + 1:]
        _mixp_kernel(*refs, L=L, n_tiles=n_tiles)

    return pl.pallas_call(
        body,
        grid=(batch, n_tiles),
        in_specs=in_specs,
        out_specs=out_specs,
        out_shape=out_shape,
        scratch_shapes=scratch,
        input_output_aliases=aliases,
        compiler_params=pltpu.CompilerParams(
            dimension_semantics=("arbitrary", "arbitrary"), vmem_limit_bytes=VMEM_LIMIT),
        name="mixer_prompt",
    )(*args)


def _store_seq8(ref, t, val):
    nb = val.shape[0]
    for k in range(ref.shape[0]):
        ref[k, pl.ds(t, nb, stride=8), :] = val[:, k * LANES:(k + 1) * LANES]


def _load_rows(ref, rows):
    return jnp.concatenate([ref[k, rows, :] for k in range(ref.shape[0])], axis=1)


def _sa_kernel(x_ref, sconv_ref, spool_ref, nw_ref, wzx_ref, wxp_ref, wdt_ref, cw_ref, cb_ref, dtb_ref, alog_ref,
               dsk_ref, pw_ref, ps_ref, e3_ref,
               z_ref, yp_ref, ypart_ref, ea_ref, xw_ref, cd3_ref, c_ref, b_ref, conv_ref, pool_ref,
               *, nb, nt):
    u = _rms(x_ref[...].reshape(nt * nb, -1), nw_ref[...]).astype(BF16)
    z_ref[...] = _dot(u, wzx_ref[:, O_Z:O_XBC]).reshape(nt, nb, D_SSD)
    xbc = _dot(u, wzx_ref[:, O_XBC:O_XP])
    xp = _dot(u, wxp_ref[...])
    dtr = _dot(u, wdt_ref[...])
    e3 = e3_ref[...]

    def tile(v, t):
        return v[t * nb:(t + 1) * nb, :]

    cext = [sconv_ref[k] for k in range(CONV_W - 1)]
    cext += [tile(xbc, t) for t in range(nt)]
    cw = cw_ref[...]
    xs, bm, cm = [], [], []
    for t in range(nt):
        acc = cb_ref[...]
        for k in range(CONV_W):
            acc = acc + cw[k:k + 1, :] * cext[t + k]
        v = _silu(acc)
        xs.append(v[:, 0:D_SSD])
        bm.append(v[:, D_SSD:D_SSD + N_GROUPS * D_STATE])
        cm.append(v[:, D_SSD + N_GROUPS * D_STATE:])
    for k in range(CONV_W - 1):
        conv_ref[:, k, :] = cext[nt + k]

    dt = _softplus(dtr + dtb_ref[...])
    a_row = -jnp.exp(alog_ref[...])
    dts = [tile(dt, t) for t in range(nt)]
    acum = []
    for t in range(nt):
        da = dts[t] * a_row
        acum.append(da if t == 0 else acum[-1] + da)
    a_last = acum[-1]

    for r in (xw_ref, cd3_ref, c_ref, b_ref):
        r[...] = jnp.zeros(r.shape, F32)
    lane = lax.broadcasted_iota(jnp.int32, (nb, LANES), 1)
    g0_heads = lane < HEADS_PER_GROUP
    dsk = dsk_ref[...]
    for t in range(nt):
        y = dsk * xs[t]
        for s in range(t + 1):
            sc = []
            for g in range(N_GROUPS):
                cg = cm[t][:, g * D_STATE:(g + 1) * D_STATE]
                bg = bm[s][:, g * D_STATE:(g + 1) * D_STATE]
                sc.append(jnp.sum(cg * bg, axis=1, keepdims=True))
            wts = jnp.where(g0_heads, sc[0], sc[1]) * jnp.exp(acum[t] - acum[s]) * dts[s]
            y = y + _expand_heads(wts, e3) * xs[s]
        ypart_ref[t] = y
        ea_ref[t] = _expand_heads(jnp.exp(acum[t]), e3)
        _store_seq8(xw_ref, t, _expand_heads(dts[t] * jnp.exp(a_last - acum[t]), e3) * xs[t])
        _store_seq8(c_ref, t, cm[t])
        _store_seq8(b_ref, t, bm[t])
    cd = jnp.exp(_expand_heads(a_last, e3))
    hi = cd.astype(BF16).astype(F32)
    mid = (cd - hi).astype(BF16).astype(F32)
    lo = (cd - hi - mid).astype(BF16).astype(F32)
    _store_seq8(cd3_ref, 0, hi)
    _store_seq8(cd3_ref, 1, mid)
    _store_seq8(cd3_ref, 2, lo)

    pext = [spool_ref[k] for k in range(POOL_BUF)]
    pext += [tile(xp, t) for t in range(nt)]
    for t in range(nt):
        yps = []
        for gi, w in enumerate(POOL_WINDOWS):
            cols = slice(gi * POOL_GD, (gi + 1) * POOL_GD)
            s = pext[POOL_BUF + t][:, cols]
            for k in range(1, w):
                s = s + pext[POOL_BUF + t - k][:, cols]
            d = s / float(w) - pext[POOL_BUF + t][:, cols]
            yps.append(_dot(d.astype(BF16), pw_ref[gi]))
        yp_ref[t] = (jnp.concatenate(yps, axis=1) * ps_ref[...]).astype(BF16)
    for k in range(POOL_BUF):
        pool_ref[:, k, :] = pext[nt + k]


def _sa_call(x_dec, sconv, spool, layer, w, depth, sb, conv_prev, pool_prev):
    nt, nb, d = x_dec.shape
    in_specs = [
        pl.BlockSpec((nt, sb, d), lambda i: (0, i, 0)),
        pl.BlockSpec((None, CONV_W - 1, sb, CONV_DIM), lambda i: (layer, 0, i, 0)),
        pl.BlockSpec((None, POOL_BUF, sb, POOL_W), lambda i: (layer, 0, i, 0)),
        _resident((1, d), layer),
        _resident((d, O_XP), layer),
        _resident((d, O_DT - O_XP), layer),
        _resident((d, O_END - O_DT), layer),
        _resident((CONV_W, CONV_DIM), layer),
        _resident((1, CONV_DIM), layer),
        _resident((1, LANES), layer),
        _resident((1, LANES), layer),
        _resident((1, D_SSD), layer),
        _resident((len(POOL_WINDOWS), POOL_GD, POOL_GD), layer),
        _resident((1, POOL_W), layer),
        _resident((3 * LANES, D_SSD)),
    ]
    args = [x_dec, sconv, spool, w["norm_mix"], w["w_zx"], w["w_xp"], w["w_dt"], w["conv_w"], w["conv_b"],
            w["dt_bias"], w["a_log"], w["d_skip"], w["pool_w"], w["pool_scale"], w["e3"]]
    n_in = len(args)
    aliases = {}
    if conv_prev is not None:
        in_specs += [pl.BlockSpec(memory_space=pl.ANY), pl.BlockSpec(memory_space=pl.ANY)]
        aliases = {n_in: 8, n_in + 1: 9}
        args += [conv_prev, pool_prev]
    gn = N_GROUPS * D_STATE
    tm_outs = [(D_SSD, F32), (POOL_W, BF16), (D_SSD, F32), (D_SSD, F32)]
    out_specs = [pl.BlockSpec((nt, sb, wd), lambda i: (0, i, 0)) for wd, _ in tm_outs]
    out_shape = [jax.ShapeDtypeStruct((nt, nb, wd), dt) for wd, dt in tm_outs]
    for wd in (D_SSD, D_SSD, gn, gn):
        out_specs.append(pl.BlockSpec((wd // LANES, sb * 8, LANES), lambda i: (0, i, 0)))
        out_shape.append(jax.ShapeDtypeStruct((wd // LANES, nb * 8, LANES), F32))
    for n_rows, wd in ((CONV_W - 1, CONV_DIM), (POOL_BUF, POOL_W)):
        out_specs.append(pl.BlockSpec((None, sb, n_rows, wd), lambda i: (layer, i, 0, 0)))
        out_shape.append(jax.ShapeDtypeStruct((depth, nb, n_rows, wd), F32))

    def body(*refs):
        if conv_prev is not None:
            refs = refs[:n_in] + refs[n_in + 2:]
        _sa_kernel(*refs, nb=sb, nt=nt)

    return pl.pallas_call(
        body,
        grid=(nb // sb,),
        in_specs=in_specs,
        out_specs=out_specs,
        out_shape=out_shape,
        input_output_aliases=aliases,
        compiler_params=pltpu.CompilerParams(
            dimension_semantics=("arbitrary",), vmem_limit_bytes=VMEM_LIMIT),
        name="mixer_decode_tokens",
    )(*args)


def _sb_kernel(h_ref, c_ref, b_ref, xw_ref, cd3_ref, hn_ref, ch_ref, *, sblk):
    half = D_SSD // N_GROUPS
    ones = jnp.ones((8, D_STATE), BF16)
    for j in range(sblk):
        rows = slice(j * 8, (j + 1) * 8)
        h0 = h_ref[j]
        hb = h0.astype(BF16)
        cj = _load_rows(c_ref, rows).astype(BF16)
        bj = _load_rows(b_ref, rows).astype(BF16)
        xw = _load_rows(xw_ref, rows).astype(BF16)
        dmat = _dot_t0(_load_rows(cd3_ref, rows).astype(BF16), ones)
        chs, upds = [], []
        for g in range(N_GROUPS):
            rs = slice(g * half, (g + 1) * half)
            ns = slice(g * D_STATE, (g + 1) * D_STATE)
            chs.append(_dot_t1(cj[:, ns], hb[rs, :]))
            upds.append(_dot_t0(xw[:, rs], bj[:, ns]))
        ch = jnp.concatenate(chs, axis=1)
        for k in range(ch_ref.shape[0]):
            ch_ref[k, rows, :] = ch[:, k * LANES:(k + 1) * LANES]
        hn_ref[j] = h0 * dmat + jnp.concatenate(upds, axis=0)


def _sb_call(h0, c8, b8, xw8, cd38, layer, depth, nb, sblk, ssm_prev):
    gn = N_GROUPS * D_STATE
    in_specs = [
        pl.BlockSpec((None, sblk, D_SSD, D_STATE), lambda i: (layer, i, 0, 0)),
        pl.BlockSpec((gn // LANES, sblk * 8, LANES), lambda i: (0, i, 0)),
        pl.BlockSpec((gn // LANES, sblk * 8, LANES), lambda i: (0, i, 0)),
        pl.BlockSpec((D_SSD // LANES, sblk * 8, LANES), lambda i: (0, i, 0)),
        pl.BlockSpec((D_SSD // LANES, sblk * 8, LANES), lambda i: (0, i, 0)),
    ]
    args = [h0, c8, b8, xw8, cd38]
    n_in = len(args)
    aliases = {}
    if ssm_prev is not None:
        in_specs.append(pl.BlockSpec(memory_space=pl.ANY))
        aliases = {n_in: 0}
        args.append(ssm_prev)

    def body(*refs):
        if ssm_prev is not None:
            refs = refs[:n_in] + refs[n_in + 1:]
        _sb_kernel(*refs, sblk=sblk)

    return pl.pallas_call(
        body,
        grid=(nb // sblk,),
        in_specs=in_specs,
        out_specs=[
            pl.BlockSpec((None, sblk, D_SSD, D_STATE), lambda i: (layer, i, 0, 0)),
            pl.BlockSpec((D_SSD // LANES, sblk * 8, LANES), lambda i: (0, i, 0)),
        ],
        out_shape=[
            jax.ShapeDtypeStruct((depth, nb, D_SSD, D_STATE), F32),
            jax.ShapeDtypeStruct((D_SSD // LANES, nb * 8, LANES), F32),
        ],
        input_output_aliases=aliases,
        compiler_params=pltpu.CompilerParams(
            dimension_semantics=("parallel",), vmem_limit_bytes=VMEM_LIMIT),
        name="mixer_decode_state",
    )(*args)


def _sc_kernel(x_ref, z_ref, ypart_ref, ea_ref, ch_ref, yp_ref, snw_ref, wout_ref, o_ref, *, nb, nt):
    ch =jnp.concatenate([_load_rows(ch_ref, pl.ds(t, nb, stride=8)) for t in range(nt)], axis=0)
    y = ypart_ref[...] + ea_ref[...] * ch
    yn = _rms(y * _silu(z_ref[...]), snw_ref[...]).astype(BF16)
    out = _dot(jnp.concatenate([yn, yp_ref[...]], axis=1), wout_ref[...])
    o_ref[...] = x_ref[...] + out


def _sc_call(x, z, ypart, ea, ch8, yp, layer, w, nb, nt):
    t_all, d = x.shape
    rows = nt * nb
    xblk = (t_all - rows) // rows

    def full(shape):
        return pl.BlockSpec(shape, lambda i: (0, 0))

    return pl.pallas_call(
        functools.partial(_sc_kernel, nb=nb, nt=nt),
        grid=(1,),
        in_specs=[pl.BlockSpec((rows, d), lambda i: (xblk, 0)), full((rows, D_SSD)), full((rows, D_SSD)),
                  full((rows, D_SSD)), pl.BlockSpec((D_SSD // LANES, nb * 8, LANES), lambda i: (0, 0, 0)),
                  full((rows, POOL_W)),
                  _resident((1, D_SSD), layer), _resident((D_SSD + POOL_W, d), layer)],
        out_specs=pl.BlockSpec((rows, d), lambda i: (0, 0)),
        out_shape=jax.ShapeDtypeStruct((rows, d), F32),
        compiler_params=pltpu.CompilerParams(vmem_limit_bytes=VMEM_LIMIT),
        name="mixer_decode_out",
    )(x, z, ypart, ea, ch8, yp, w["ssd_norm_w"], w["w_out"])


def _prep_weights(w_in, conv_w, conv_b, dt_bias, a_log, d_skip, ssd_norm_w, pool_w, pool_scale, w_out,
                  norm_mix):
    depth = w_in.shape[0]
    o1, o2, o3 = D_SSD, D_SSD + CONV_DIM, D_SSD + CONV_DIM + N_HEADS
    w_dt = jnp.pad(w_in[:, :, o2:o3], ((0, 0), (0, 0), (0, LANES - N_HEADS)))
    pad_h = ((0, 0), (0, LANES - N_HEADS))
    head_of_lane = jnp.arange(D_SSD) // HEAD_DIM
    e1 = (jnp.arange(LANES)[:, None] == head_of_lane[None, :]).astype(BF16)
    return {
        "norm_mix": norm_mix.reshape(depth, 1, -1),
        "w_zx": w_in[:, :, :o2].astype(BF16),
        "w_xp": w_in[:, :, o3:].astype(BF16),
        "w_dt": w_dt.astype(BF16),
        "conv_w": conv_w,
        "conv_b": conv_b.reshape(depth, 1, -1),
        "dt_bias": jnp.pad(dt_bias, pad_h).reshape(depth, 1, LANES),
        "a_log": jnp.pad(a_log, pad_h).reshape(depth, 1, LANES),
        "d_skip": jnp.repeat(d_skip, HEAD_DIM, axis=-1).reshape(depth, 1, D_SSD),
        "ssd_norm_w": ssd_norm_w.reshape(depth, 1, -1),
        "pool_w": pool_w.astype(BF16),
        "pool_scale": pool_scale.reshape(depth, 1, -1),
        "w_out": w_out.astype(BF16),
        "e3": jnp.concatenate([e1, e1, e1], axis=0),
    }


def kernel(x_prompt, x_sample, p_prompt, p_sample, state_ssm, state_conv, state_pool, w_in, conv_w, conv_b,
           dt_bias, a_log, d_skip, ssd_norm_w, pool_w, pool_scale, w_out, norm_ffn1, ffn1_gate, ffn1_up,
           ffn1_down, norm_mix, norm_ffn2, ffn2_gate, ffn2_up, ffn2_down, norm_ple, ple_gate, ple_proj,
           final_norm):
    L, sblk = MIX_TILE, DEC_SEQ_BLOCK
    depth = w_in.shape[0]
    batch, seq, d = x_prompt.shape
    nb, nt, _ = x_sample.shape
    n_prompt, n_dec = batch * seq, nt * nb
    tm = n_dec
    n_main = n_prompt // tm
    assert n_main * tm == n_prompt
    mw = _prep_weights(w_in, conv_w, conv_b, dt_bias, a_log, d_skip, ssd_norm_w, pool_w, pool_scale, w_out,
                       norm_mix)
    f1 = (norm_ffn1.reshape(depth, 1, d), ffn1_gate, ffn1_up, ffn1_down)
    f2 = (norm_ffn2.reshape(depth, 1, d), ffn2_gate, ffn2_up, ffn2_down)
    ple = (p_prompt.reshape(depth, n_prompt, -1), jnp.transpose(p_sample, (0, 2, 1, 3)).reshape(depth, n_dec, -1),
           norm_ple.reshape(depth, 1, d), ple_gate.astype(BF16), ple_proj.astype(BF16), final_norm.reshape(1, d))
    x = (x_prompt.reshape(n_prompt, d), jnp.transpose(x_sample, (1, 0, 2)).reshape(n_dec, d))
    h_all = state_ssm.reshape(depth, nb, D_SSD, D_STATE)
    sconv_tm = jnp.transpose(state_conv, (0, 2, 1, 3))
    spool_tm = jnp.transpose(state_pool, (0, 2, 1, 3))

    ssm_p = ssm_s = conv_s = pool_s = None
    conv_p, pool_p = [], []
    for i in range(depth):
        last = i == depth - 1
        x = _ffn_call(x, i, *f1, tm, n_main)
        xm, ssm_p, cp, plp = _mixp_call(x, i, mw, depth, batch, seq, L, ssm_p)
        z, yp, ypart, ea, xw8, cd38, c8, b8, conv_s, pool_s = _sa_call(
            x[n_prompt:].reshape(nt, nb, d), sconv_tm, spool_tm, i, mw, depth, DEC_TOKEN_SEQ_BLOCK, conv_s, pool_s)
        z, yp, ypart, ea = (v.reshape(n_dec, -1) for v in (z, yp, ypart, ea))
        ssm_s, ch8 = _sb_call(h_all, c8, b8, xw8, cd38, i, depth, nb, sblk, ssm_s)
        xd = _sc_call(x, z, ypart, ea, ch8, yp, i, mw, nb, nt)
        x = _ffn_call((xm, xd), i, *f2, tm, n_main, ple=ple, final=last, split_out=last)
        conv_p.append(cp)
        pool_p.append(plp)
    y_prompt = x[0].reshape(batch, seq, d)
    y_sample = jnp.transpose(x[1].reshape(nt, nb, d), (1, 0, 2))
    return (y_prompt, y_sample,
            ssm_p.reshape(depth, batch, N_HEADS, HEAD_DIM, D_STATE), jnp.stack(conv_p), jnp.stack(pool_p),
            ssm_s.reshape(depth, nb, N_HEADS, HEAD_DIM, D_STATE),
            conv_s, pool_s)
```

```python
import functools

import jax
import jax.numpy as jnp
from jax import lax
from jax.experimental import pallas as pl
from jax.experimental.pallas import tpu as pltpu

F32 = jnp.float32
BF16 = jnp.bfloat16

EPS = 1e-6
D_MODEL = 1024
D_SSD = 1024
HEAD_DIM = 64
N_HEADS = 16
N_GROUPS = 2
HEADS_PER_GROUP = 8
D_STATE = 128
CONV_W = 4
CONV_DIM = D_SSD + 2 * N_GROUPS * D_STATE
CHUNK = 128
POOL_WINDOWS = (2, 4, 8, 16)
POOL_GD = 256
POOL_BUF = 15
POOL_W = 1024
LANES = 128
O_Z, O_XBC, O_XP, O_DT, O_END = 0, 1024, 2560, 3584, 3712
VMEM_LIMIT = 56 * 1024 * 1024
D_IN_PROJ = D_SSD + CONV_DIM + N_HEADS + POOL_W
MIX_TILE = 256
DEC_SEQ_BLOCK = 8
DEC_TOKEN_SEQ_BLOCK = 64


def _dot(a, b):
    return jnp.dot(a, b, preferred_element_type=F32)


def _dot_t0(a, b):
    return lax.dot_general(a, b, (((0,), (0,)), ((), ())), preferred_element_type=F32)


def _dot_t1(a, b):
    return lax.dot_general(a, b, (((1,), (1,)), ((), ())), preferred_element_type=F32)


def _rms(x, w):
    return x * lax.rsqrt(jnp.mean(x * x, axis=-1, keepdims=True) + EPS) * w


def _silu(x):
    return x * jax.nn.sigmoid(x)


def _softplus(x):
    return jnp.maximum(x, 0.0) + jnp.log1p(jnp.exp(-jnp.abs(x)))


def _split3(v):
    hi = v.astype(BF16)
    r1 = v - hi.astype(F32)
    mid = r1.astype(BF16)
    lo = (r1 - mid.astype(F32)).astype(BF16)
    return jnp.concatenate([hi, mid, lo], axis=1)


def _expand_heads(v, e3):
    return _dot(_split3(v), e3)


FF_CHUNK = 512


def _ffn_kernel(*refs, layer, ple, final, n_main, split_in, split_out):
    refs = list(refs)
    step = pl.program_id(0)
    is_main = step < n_main
    if split_in:
        xa_ref, xb_ref = refs[:2]
        x = jnp.where(is_main, xa_ref[...], xb_ref[...])
        refs = refs[2:]
    else:
        x = refs[0][...]
        refs = refs[1:]
    nw_ref, wg_hbm, wu_hbm, wd_hbm = refs[:4]
    refs = refs[4:]
    if ple:
        pa_ref, pb_ref, npw_ref, pg_ref, pp_ref, fn_ref = refs[:6]
        refs = refs[6:]
    wg_sc, wu_sc, wd_sc, stage_g, stage_u, stage_d, sem = refs[-7:]
    refs = refs[:-7]
    ff = wg_sc.shape[-1]
    chunks = [(c0, min(c0 + FF_CHUNK, ff)) for c0 in range(0, ff, FF_CHUNK)]

    def copies(c, slot):
        c0, c1 = chunks[c]
        w = c1 - c0
        return (pltpu.make_async_copy(wg_hbm.at[layer, :, c0:c1], stage_g.at[slot, :, 0:w], sem.at[0, slot]),
                pltpu.make_async_copy(wu_hbm.at[layer, :, c0:c1], stage_u.at[slot, :, 0:w], sem.at[1, slot]),
                pltpu.make_async_copy(wd_hbm.at[layer, c0:c1, :], stage_d.at[slot, 0:w, :], sem.at[2, slot]))

    def chunk_product(u, c):
        c0, c1 = chunks[c]
        g = _dot(u, wg_sc[:, c0:c1])
        up = _dot(u, wu_sc[:, c0:c1])
        return _dot((_silu(g) * up).astype(BF16), wd_sc[c0:c1, :])

    def tile(first):
        u = _rms(x, nw_ref[...]).astype(BF16)
        if first:
            for cp in copies(0, 0):
                cp.start()
        acc = None
        for c, (c0, c1) in enumerate(chunks):
            if first:
                slot = c % 2
                if c + 1 < len(chunks):
                    for cp in copies(c + 1, 1 - slot):
                        cp.start()
                for cp in copies(c, slot):
                    cp.wait()
                w = c1 - c0
                wg_sc[:, c0:c1] = stage_g[slot, :, 0:w].astype(BF16)
                wu_sc[:, c0:c1] = stage_u[slot, :, 0:w].astype(BF16)
                wd_sc[c0:c1, :] = stage_d[slot, 0:w, :].astype(BF16)
            d = chunk_product(u, c)
            acc = d if acc is None else acc + d
        x1 = x + 0.5 * acc
        if ple:
            un = _rms(x1, npw_ref[...]).astype(BF16)
            gate = jax.nn.sigmoid(_dot(un, pg_ref[...]))
            p = jnp.where(is_main, pa_ref[...], pb_ref[...]).astype(BF16)
            x1 = x1 + gate * _dot(p, pp_ref[...])
            if final:
                x1 = _rms(x1, fn_ref[...])
        if split_out:
            oa_ref, ob_ref = refs

            @pl.when(is_main)
            def _():
                oa_ref[...] = x1

            @pl.when(jnp.logical_not(is_main))
            def _():
                ob_ref[...] = x1
        else:
            refs[0][...] = x1

    pl.when(step == 0)(functools.partial(tile, True))
    pl.when(step > 0)(functools.partial(tile, False))


def _resident(shape, layer=None):
    n = len(shape)
    if layer is None:
        return pl.BlockSpec(shape, lambda *_: (0,) * n, pipeline_mode=pl.Buffered(1))
    return pl.BlockSpec((None,) + shape, lambda *_: (layer,) + (0,) * n, pipeline_mode=pl.Buffered(1))


def _ffn_call(x, layer, nw, wg, wu, wd, tm, n_main, ple=None, final=False, split_out=False):
    split_in = isinstance(x, tuple)
    d = wg.shape[1]
    ff = wg.shape[-1]

    def main_blk(i):
        return (jnp.minimum(i, n_main - 1), 0)

    def tail_blk(i):
        return (0, 0)

    if split_in:
        assert x[0].shape == (n_main * tm, d) and x[1].shape == (tm, d)
        in_specs = [pl.BlockSpec((tm, d), main_blk), pl.BlockSpec((tm, d), tail_blk)]
        args = list(x)
    else:
        assert x.shape == ((n_main + 1) * tm, d)
        in_specs = [pl.BlockSpec((tm, d), lambda i: (i, 0))]
        args = [x]
    in_specs += [_resident((1, d), layer)] + [pl.BlockSpec(memory_space=pl.ANY)] * 3
    args += [nw, wg, wu, wd]
    if ple is not None:
        pa, pb, npw, pg, pp, fn = ple
        pd = pa.shape[-1]
        in_specs += [
            pl.BlockSpec((None, tm, pd), lambda i: (layer, jnp.minimum(i, n_main - 1), 0)),
            pl.BlockSpec((None, tm, pd), lambda i: (layer, 0, 0)),
            _resident((1, d), layer),
            _resident((d, d), layer),
            _resident((pd, d), layer),
            _resident((1, d)),
        ]
        args += [pa, pb, npw, pg, pp, fn]
    if split_out:
        out_specs = [pl.BlockSpec((tm, d), main_blk), pl.BlockSpec((tm, d), tail_blk)]
        out_shape = [jax.ShapeDtypeStruct((n_main * tm, d), F32), jax.ShapeDtypeStruct((tm, d), F32)]
    else:
        out_specs = pl.BlockSpec((tm, d), lambda i: (i, 0))
        out_shape = jax.ShapeDtypeStruct(((n_main + 1) * tm, d), F32)
    return pl.pallas_call(
        functools.partial(_ffn_kernel, layer=layer, ple=ple is not None, final=final, n_main=n_main,
                          split_in=split_in, split_out=split_out),
        grid=(n_main + 1,),
        in_specs=in_specs,
        out_specs=out_specs,
        out_shape=out_shape,
        scratch_shapes=[
            pltpu.VMEM((d, ff), BF16), pltpu.VMEM((d, ff), BF16), pltpu.VMEM((ff, d), BF16),
            pltpu.VMEM((2, d, FF_CHUNK), F32), pltpu.VMEM((2, d, FF_CHUNK), F32), pltpu.VMEM((2, FF_CHUNK, d), F32),
            pltpu.SemaphoreType.DMA((3, 2))],
        compiler_params=pltpu.CompilerParams(
            dimension_semantics=("arbitrary",), vmem_limit_bytes=VMEM_LIMIT),
        name="ffn_ple" if ple is not None else "ffn",
    )(*args)


PROJ_CHUNK = 256


def _proj_chunks(pieces):
    out, dst = [], 0
    for w_ref, start, width in pieces:
        for lo in range(0, width, PROJ_CHUNK):
            hi = min(lo + PROJ_CHUNK, width)
            out.append((w_ref, start + lo, start + hi, dst + lo))
        dst += width
    return out


def _proj_pieces(w_ref, wxp_sc):
    return ((w_ref, 0, O_XP), (wxp_sc, 0, O_DT - O_XP), (w_ref, O_XP, O_END - O_DT))


def _in_proj(x_ref, nw_ref, w_refs, proj_ref):
    u = _rms(x_ref[...], nw_ref[...]).astype(BF16)
    for w_ref, lo, hi, dst in _proj_chunks(w_refs):
        proj_ref[:, dst:dst + hi - lo] = _dot(u, w_ref[:, lo:hi])


class _Filler:
    def __init__(self, u_ref, w_refs, dst_ref):
        self.u_ref, self.dst_ref = u_ref, dst_ref
        self.todo = _proj_chunks(w_refs)

    def emit(self, n=1):
        for _ in range(n):
            if self.todo:
                w_ref, lo, hi, dst = self.todo.pop(0)
                self.dst_ref[:, dst:dst + hi - lo] = _dot(self.u_ref[...], w_ref[:, lo:hi])

    def drain(self):
        self.emit(len(self.todo))


def _ssd_chunk(c, dt_sc, xbc_sc, y_sc, h_sc, a_row, dsk, e3, tril, causal, lo_mask, filler):
    rows = slice(c * CHUNK, (c + 1) * CHUNK)
    dtc = dt_sc[rows, :]
    acum = _dot(tril, _split3(dtc * a_row))
    acum = acum[:, 0:LANES] + acum[:, LANES:2 * LANES] + acum[:, 2 * LANES:3 * LANES]
    a_t = acum.T[0:N_HEADS, :]
    dt_t = dtc.T[0:N_HEADS, :]
    w_t = dt_t * jnp.exp(a_t[:, CHUNK - 1:CHUNK] - a_t)
    cd = jnp.exp(_expand_heads(acum[CHUNK - 8:CHUNK, :], e3)[7:8, :])
    for g in range(N_GROUPS):
        b_g = xbc_sc[rows, D_SSD + g * D_STATE:D_SSD + (g + 1) * D_STATE]
        c_g = xbc_sc[rows, D_SSD + (N_GROUPS + g) * D_STATE:D_SSD + (N_GROUPS + g + 1) * D_STATE]
        bt_g = b_g.T
        s_g = _dot(c_g.astype(BF16), bt_g.astype(BF16))
        for k in range(HEADS_PER_GROUP // 2):
            lb = g * (HEADS_PER_GROUP // 2) + k
            cols = slice(lb * LANES, (lb + 1) * LANES)
            lhs, btw = [], []
            for r in (2 * lb, 2 * lb + 1):
                lq = acum[:, r:r + 1]
                dec = jnp.where(causal, jnp.exp(lq - a_t[r:r + 1, :]), 0.0)
                lhs.append((s_g * dec * dt_t[r:r + 1, :]).astype(BF16))
                lhs.append((c_g * jnp.exp(lq)).astype(BF16))
                btw.append((bt_g * w_t[r:r + 1, :]).astype(BF16))
            xs = xbc_sc[rows, cols]
            hp = h_sc[:, cols]
            x_lo = jnp.where(lo_mask, xs, 0.0).astype(BF16)
            x_hi = jnp.where(lo_mask, 0.0, xs).astype(BF16)
            h_lo = jnp.where(lo_mask, hp, 0.0).astype(BF16)
            h_hi = jnp.where(lo_mask, 0.0, hp).astype(BF16)
            y = _dot(jnp.concatenate(lhs, axis=1), jnp.concatenate([x_lo, h_lo, x_hi, h_hi], axis=0))
            y_sc[rows, cols] = y + dsk[:, cols] * xs
            upd = _dot(jnp.concatenate(btw, axis=1), jnp.concatenate([x_lo, x_hi], axis=0))
            h_sc[:, cols] = hp * cd[:, cols] + upd
            filler.emit()


def _pool_group_sums(ext, gi, w):
    s = ext[:, gi * POOL_GD:(gi + 1) * POOL_GD]
    sh = 1
    while sh < w:
        s = s + pltpu.roll(s, sh, 0)
        sh *= 2
    return s[16:, :]


def _mixp_kernel(x_ref, xn_ref, nw_ref, w_ref, cw_ref, cb_ref, dtb_ref, alog_ref, dsk_ref,
                 snw_ref, pw_ref, ps_ref, wout_ref, e3_ref,
                 xo_ref, ssm_ref, conv_ref, pool_ref,
                 h_sc, cext_sc, pext_sc, xbc_sc, dt_sc, y_sc, u_sc, proj_a, proj_b, wxp_sc, *, L, n_tiles):
    i = pl.program_id(1)
    g = pl.program_id(0) * n_tiles + i
    w_refs = _proj_pieces(w_ref, wxp_sc)

    @pl.when(g == 0)
    def _():
        wxp_sc[...] = w_ref[:, O_XP + N_HEADS:O_XP + N_HEADS + POOL_W]
        _in_proj(x_ref, nw_ref, w_refs, proj_a)

    @pl.when(i == 0)
    def _():
        h_sc[...] = jnp.zeros_like(h_sc)
        cext_sc[0:8, :] = jnp.zeros((8, CONV_DIM), F32)
        pext_sc[0:16, :] = jnp.zeros((16, POOL_W), F32)

    tile = functools.partial(
        _mixp_tile, i, x_ref, xn_ref, nw_ref, w_refs, cw_ref, cb_ref, dtb_ref, alog_ref, dsk_ref, snw_ref, pw_ref,
        ps_ref, wout_ref, e3_ref, xo_ref, h_sc, cext_sc, pext_sc, xbc_sc, dt_sc, y_sc, u_sc, L=L)

    @pl.when(g % 2 == 0)
    def _():
        tile(proj_a, proj_b)

    @pl.when(g % 2 == 1)
    def _():
        tile(proj_b, proj_a)

    @pl.when(i == n_tiles - 1)
    def _():
        ssm_ref[...] = h_sc[...].T
        conv_ref[...] = cext_sc[L + 5:L + 8, :]
        pool_ref[...] = pext_sc[L + 1:L + 16, :]

    cext_sc[0:8, :] = cext_sc[L:L + 8, :]
    pext_sc[0:16, :] = pext_sc[L:L + 16, :]


def _mixp_tile(i, x_ref, xn_ref, nw_ref, w_refs, cw_ref, cb_ref, dtb_ref, alog_ref, dsk_ref, snw_ref, pw_ref,
               ps_ref, wout_ref, e3_ref, xo_ref, h_sc, cext_sc, pext_sc, xbc_sc, dt_sc, y_sc, u_sc, proj_cur, proj_nxt,
               *, L):
    u_sc[...] = _rms(xn_ref[...], nw_ref[...]).astype(BF16)
    filler = _Filler(u_sc, w_refs, proj_nxt)

    cext_sc[8:8 + L, :] = proj_cur[:, O_XBC:O_XP]
    cw = cw_ref[...]
    cb = cb_ref[...]
    for c0 in range(0, CONV_DIM, PROJ_CHUNK):
        cs = slice(c0, c0 + PROJ_CHUNK)
        conv = cb[:, cs] + cw[3:4, cs] * cext_sc[8:8 + L, cs]
        for k in range(CONV_W - 1):
            conv = conv + cw[k:k + 1, cs] * cext_sc[5 + k:5 + k + L, cs]
        xbc_sc[:, cs] = _silu(conv)
        filler.emit()

    dt_sc[...] = _softplus(proj_cur[:, O_DT:O_END] + dtb_ref[...])
    a_row = -jnp.exp(alog_ref[...])
    filler.emit()

    ri = lax.broadcasted_iota(jnp.int32, (CHUNK, CHUNK), 0)
    ci = lax.broadcasted_iota(jnp.int32, (CHUNK, CHUNK), 1)
    causal = ri >= ci
    tril = causal.astype(BF16)
    lo_mask = ci < HEAD_DIM
    dsk = dsk_ref[...]
    e3 = e3_ref[...]
    xp = proj_cur[:, O_XP:O_DT]
    pext_sc[16:16 + L, :] = xp
    ext = pext_sc[...]
    t_abs = i * L + lax.broadcasted_iota(jnp.int32, (L, 1), 0)
    yps = []
    for gi, w in enumerate(POOL_WINDOWS):
        cnt = jnp.minimum(t_abs + 1, w).astype(F32)
        d = _pool_group_sums(ext, gi, w) / cnt - xp[:, gi * POOL_GD:(gi + 1) * POOL_GD]
        yps.append(_dot(d.astype(BF16), pw_ref[gi]))
    yp = (jnp.concatenate(yps, axis=1) * ps_ref[...]).astype(BF16)
    out_pool = _dot(yp, wout_ref[D_SSD:D_SSD + POOL_W, :])

    for c in range(L // CHUNK):
        _ssd_chunk(c, dt_sc, xbc_sc, y_sc, h_sc, a_row, dsk, e3, tril, causal, lo_mask, filler)
    filler.drain()

    yn = _rms(y_sc[...] * _silu(proj_cur[:, O_Z:O_XBC]), snw_ref[...]).astype(BF16)
    xo_ref[...] = x_ref[...] + (out_pool + _dot(yn, wout_ref[0:D_SSD, :]))


def _mixp_call(x, layer, w, depth, batch, seq, L, ssm_prev):
    n_tiles = seq // L
    d = x.shape[-1]
    last = batch * n_tiles - 1
    in_specs = [
        pl.BlockSpec((L, d), lambda b, i: (b * n_tiles + i, 0)),
        pl.BlockSpec((L, d), lambda b, i: (jnp.minimum(b * n_tiles + i + 1, last), 0)),
        _resident((1, d), layer),
        _resident((d, D_IN_PROJ), layer),
        _resident((CONV_W, CONV_DIM), layer),
        _resident((1, CONV_DIM), layer),
        _resident((1, LANES), layer),
        _resident((1, LANES), layer),
        _resident((1, D_SSD), layer),
        _resident((1, D_SSD), layer),
        _resident((len(POOL_WINDOWS), POOL_GD, POOL_GD), layer),
        _resident((1, POOL_W), layer),
        _resident((D_SSD + POOL_W, d), layer),
        _resident((3 * LANES, D_SSD)),
    ]
    args = [x, x, w["norm_mix"], w["w_in"], w["conv_w"], w["conv_b"], w["dt_bias"],
            w["a_log"], w["d_skip"], w["ssd_norm_w"], w["pool_w"], w["pool_scale"], w["w_out"], w["e3"]]
    n_in = len(args)
    aliases = {}
    if ssm_prev is not None:
        in_specs.append(pl.BlockSpec(memory_space=pl.ANY))
        aliases = {n_in: 1}
        args.append(ssm_prev)
    out_specs = [
        pl.BlockSpec((L, d), lambda b, i: (b * n_tiles + i, 0)),
        pl.BlockSpec((None, None, D_SSD, D_STATE), lambda b, i: (layer, b, 0, 0)),
        pl.BlockSpec((None, CONV_W - 1, CONV_DIM), lambda b, i: (b, 0, 0)),
        pl.BlockSpec((None, POOL_BUF, POOL_W), lambda b, i: (b, 0, 0)),
    ]
    out_shape = [
        jax.ShapeDtypeStruct((batch * seq, d), F32),
        jax.ShapeDtypeStruct((depth, batch, D_SSD, D_STATE), F32),
        jax.ShapeDtypeStruct((batch, CONV_W - 1, CONV_DIM), F32),
        jax.ShapeDtypeStruct((batch, POOL_BUF, POOL_W), F32),
    ]
    scratch = [
        pltpu.VMEM((D_STATE, D_SSD), F32),
        pltpu.VMEM((8 + L, CONV_DIM), F32),
        pltpu.VMEM((16 + L, POOL_W), F32),
        pltpu.VMEM((L, CONV_DIM), F32),
        pltpu.VMEM((L, LANES), F32),
        pltpu.VMEM((L, D_SSD), F32),
        pltpu.VMEM((L, D_MODEL), BF16),
        pltpu.VMEM((L, O_END), F32),
        pltpu.VMEM((L, O_END), F32),
        pltpu.VMEM((D_MODEL, POOL_W), BF16),
    ]

    def body(*refs):
        if ssm_prev is not None:
            refs = refs[:n_in] + refs[n_in + 1:]
        _mixp_kernel(*refs, L=L, n_tiles=n_tiles)

    return pl.pallas_call(
        body,
        grid=(batch, n_tiles),
        in_specs=in_specs,
        out_specs=out_specs,
        out_shape=out_shape,
        scratch_shapes=scratch,
        input_output_aliases=aliases,
        compiler_params=pltpu.CompilerParams(
            dimension_semantics=("arbitrary", "arbitrary"), vmem_limit_bytes=VMEM_LIMIT),
        name="mixer_prompt",
    )(*args)


def _store_seq8(ref, t, val):
    nb = val.shape[0]
    for k in range(ref.shape[0]):
        ref[k, pl.ds(t, nb, stride=8), :] = val[:, k * LANES:(k + 1) * LANES]


def _load_rows(ref, rows):
    return jnp.concatenate([ref[k, rows, :] for k in range(ref.shape[0])], axis=1)


def _sa_kernel(x_ref, sconv_ref, spool_ref, nw_ref, w_ref, cw_ref, cb_ref, dtb_ref, alog_ref,
               dsk_ref, pw_ref, ps_ref, e3_ref,
               z_ref, yp_ref, ypart_ref, ea_ref, xw_ref, cd3_ref, c_ref, b_ref, conv_ref, pool_ref,
               *, nb, nt):
    u = _rms(x_ref[...].reshape(nt * nb, -1), nw_ref[...]).astype(BF16)
    z_ref[...] = _dot(u, w_ref[:, O_Z:O_XBC]).reshape(nt, nb, D_SSD)
    xbc = _dot(u, w_ref[:, O_XBC:O_XP])
    xp = _dot(u, w_ref[:, O_XP + N_HEADS:O_XP + N_HEADS + POOL_W])
    dtr = _dot(u, w_ref[:, O_XP:O_XP + LANES])
    e3 = e3_ref[...]

    def tile(v, t):
        return v[t * nb:(t + 1) * nb, :]

    cext = [sconv_ref[k] for k in range(CONV_W - 1)]
    cext += [tile(xbc, t) for t in range(nt)]
    cw = cw_ref[...]
    xs, bm, cm = [], [], []
    for t in range(nt):
        acc = cb_ref[...]
        for k in range(CONV_W):
            acc = acc + cw[k:k + 1, :] * cext[t + k]
        v = _silu(acc)
        xs.append(v[:, 0:D_SSD])
        bm.append(v[:, D_SSD:D_SSD + N_GROUPS * D_STATE])
        cm.append(v[:, D_SSD + N_GROUPS * D_STATE:])
    for k in range(CONV_W - 1):
        conv_ref[:, k, :] = cext[nt + k]

    dt = _softplus(dtr + dtb_ref[...])
    a_row = -jnp.exp(alog_ref[...])
    dts = [tile(dt, t) for t in range(nt)]
    acum = []
    for t in range(nt):
        da = dts[t] * a_row
        acum.append(da if t == 0 else acum[-1] + da)
    a_last = acum[-1]

    for r in (xw_ref, cd3_ref, c_ref, b_ref):
        r[...] = jnp.zeros(r.shape, F32)
    lane = lax.broadcasted_iota(jnp.int32, (nb, LANES), 1)
    g0_heads = lane < HEADS_PER_GROUP
    dsk = dsk_ref[...]
    for t in range(nt):
        y = dsk * xs[t]
        for s in range(t + 1):
            sc = []
            for g in range(N_GROUPS):
                cg = cm[t][:, g * D_STATE:(g + 1) * D_STATE]
                bg = bm[s][:, g * D_STATE:(g + 1) * D_STATE]
                sc.append(jnp.sum(cg * bg, axis=1, keepdims=True))
            wts = jnp.where(g0_heads, sc[0], sc[1]) * jnp.exp(acum[t] - acum[s]) * dts[s]
            y = y + _expand_heads(wts, e3) * xs[s]
        ypart_ref[t] = y
        ea_ref[t] = _expand_heads(jnp.exp(acum[t]), e3)
        _store_seq8(xw_ref, t, _expand_heads(dts[t] * jnp.exp(a_last - acum[t]), e3) * xs[t])
        _store_seq8(c_ref, t, cm[t])
        _store_seq8(b_ref, t, bm[t])
    cd = jnp.exp(_expand_heads(a_last, e3))
    hi = cd.astype(BF16).astype(F32)
    mid = (cd - hi).astype(BF16).astype(F32)
    lo = (cd - hi - mid).astype(BF16).astype(F32)
    _store_seq8(cd3_ref, 0, hi)
    _store_seq8(cd3_ref, 1, mid)
    _store_seq8(cd3_ref, 2, lo)

    pext = [spool_ref[k] for k in range(POOL_BUF)]
    pext += [tile(xp, t) for t in range(nt)]
    for t in range(nt):
        yps = []
        for gi, w in enumerate(POOL_WINDOWS):
            cols = slice(gi * POOL_GD, (gi + 1) * POOL_GD)
            s = pext[POOL_BUF + t][:, cols]
            for k in range(1, w):
                s = s + pext[POOL_BUF + t - k][:, cols]
            d = s / float(w) - pext[POOL_BUF + t][:, cols]
            yps.append(_dot(d.astype(BF16), pw_ref[gi]))
        yp_ref[t] = (jnp.concatenate(yps, axis=1) * ps_ref[...]).astype(BF16)
    for k in range(POOL_BUF):
        pool_ref[:, k, :] = pext[nt + k]


def _sa_call(x_dec, sconv, spool, layer, w, depth, sb, conv_prev, pool_prev):
    nt, nb, d = x_dec.shape
    in_specs = [
        pl.BlockSpec((nt, sb, d), lambda i: (0, i, 0)),
        pl.BlockSpec((None, CONV_W - 1, sb, CONV_DIM), lambda i: (layer, 0, i, 0)),
        pl.BlockSpec((None, POOL_BUF, sb, POOL_W), lambda i: (layer, 0, i, 0)),
        _resident((1, d), layer),
        _resident((d, D_IN_PROJ), layer),
        _resident((CONV_W, CONV_DIM), layer),
        _resident((1, CONV_DIM), layer),
        _resident((1, LANES), layer),
        _resident((1, LANES), layer),
        _resident((1, D_SSD), layer),
        _resident((len(POOL_WINDOWS), POOL_GD, POOL_GD), layer),
        _resident((1, POOL_W), layer),
        _resident((3 * LANES, D_SSD)),
    ]
    args = [x_dec, sconv, spool, w["norm_mix"], w["w_in"], w["conv_w"], w["conv_b"],
            w["dt_bias"], w["a_log"], w["d_skip"], w["pool_w"], w["pool_scale"], w["e3"]]
    n_in = len(args)
    aliases = {}
    if conv_prev is not None:
        in_specs += [pl.BlockSpec(memory_space=pl.ANY), pl.BlockSpec(memory_space=pl.ANY)]
        aliases = {n_in: 8, n_in + 1: 9}
        args += [conv_prev, pool_prev]
    gn = N_GROUPS * D_STATE
    tm_outs = [(D_SSD, F32), (POOL_W, BF16), (D_SSD, F32), (D_SSD, F32)]
    out_specs = [pl.BlockSpec((nt, sb, wd), lambda i: (0, i, 0)) for wd, _ in tm_outs]
    out_shape = [jax.ShapeDtypeStruct((nt, nb, wd), dt) for wd, dt in tm_outs]
    for wd in (D_SSD, D_SSD, gn, gn):
        out_specs.append(pl.BlockSpec((wd // LANES, sb * 8, LANES), lambda i: (0, i, 0)))
        out_shape.append(jax.ShapeDtypeStruct((wd // LANES, nb * 8, LANES), F32))
    for n_rows, wd in ((CONV_W - 1, CONV_DIM), (POOL_BUF, POOL_W)):
        out_specs.append(pl.BlockSpec((None, sb, n_rows, wd), lambda i: (layer, i, 0, 0)))
        out_shape.append(jax.ShapeDtypeStruct((depth, nb, n_rows, wd), F32))

    def body(*refs):
        if conv_prev is not None:
            refs = refs[:n_in] + refs[n_in + 2:]
        _sa_kernel(*refs, nb=sb, nt=nt)

    return pl.pallas_call(
        body,
        grid=(nb // sb,),
        in_specs=in_specs,
        out_specs=out_specs,
        out_shape=out_shape,
        input_output_aliases=aliases,
        compiler_params=pltpu.CompilerParams(
            dimension_semantics=("arbitrary",), vmem_limit_bytes=VMEM_LIMIT),
        name="mixer_decode_tokens",
    )(*args)


def _sb_kernel(h_ref, c_ref, b_ref, xw_ref, cd3_ref, hn_ref, ch_ref, *, sblk):
    half = D_SSD // N_GROUPS
    ones = jnp.ones((8, D_STATE), BF16)
    for j in range(sblk):
        rows = slice(j * 8, (j + 1) * 8)
        h0 = h_ref[j]
        hb = h0.astype(BF16)
        cj = _load_rows(c_ref, rows).astype(BF16)
        bj = _load_rows(b_ref, rows).astype(BF16)
        xw = _load_rows(xw_ref, rows).astype(BF16)
        dmat = _dot_t0(_load_rows(cd3_ref, rows).astype(BF16), ones)
        chs, upds = [], []
        for g in range(N_GROUPS):
            rs = slice(g * half, (g + 1) * half)
            ns = slice(g * D_STATE, (g + 1) * D_STATE)
            chs.append(_dot_t1(cj[:, ns], hb[rs, :]))
            upds.append(_dot_t0(xw[:, rs], bj[:, ns]))
        ch = jnp.concatenate(chs, axis=1)
        for k in range(ch_ref.shape[0]):
            ch_ref[k, rows, :] = ch[:, k * LANES:(k + 1) * LANES]
        hn_ref[j] = h0 * dmat + jnp.concatenate(upds, axis=0)


def _sb_call(h0, c8, b8, xw8, cd38, layer, depth, nb, sblk, ssm_prev):
    gn = N_GROUPS * D_STATE
    in_specs = [
        pl.BlockSpec((None, sblk, D_SSD, D_STATE), lambda i: (layer, i, 0, 0)),
        pl.BlockSpec((gn // LANES, sblk * 8, LANES), lambda i: (0, i, 0)),
        pl.BlockSpec((gn // LANES, sblk * 8, LANES), lambda i: (0, i, 0)),
        pl.BlockSpec((D_SSD // LANES, sblk * 8, LANES), lambda i: (0, i, 0)),
        pl.BlockSpec((D_SSD // LANES, sblk * 8, LANES), lambda i: (0, i, 0)),
    ]
    args = [h0, c8, b8, xw8, cd38]
    n_in = len(args)
    aliases = {}
    if ssm_prev is not None:
        in_specs.append(pl.BlockSpec(memory_space=pl.ANY))
        aliases = {n_in: 0}
        args.append(ssm_prev)

    def body(*refs):
        if ssm_prev is not None:
            refs = refs[:n_in] + refs[n_in + 1:]
        _sb_kernel(*refs, sblk=sblk)

    return pl.pallas_call(
        body,
        grid=(nb // sblk,),
        in_specs=in_specs,
        out_specs=[
            pl.BlockSpec((None, sblk, D_SSD, D_STATE), lambda i: (layer, i, 0, 0)),
            pl.BlockSpec((D_SSD // LANES, sblk * 8, LANES), lambda i: (0, i, 0)),
        ],
        out_shape=[
            jax.ShapeDtypeStruct((depth, nb, D_SSD, D_STATE), F32),
            jax.ShapeDtypeStruct((D_SSD // LANES, nb * 8, LANES), F32),
        ],
        input_output_aliases=aliases,
        compiler_params=pltpu.CompilerParams(
            dimension_semantics=("parallel",), vmem_limit_bytes=VMEM_LIMIT),
        name="mixer_decode_state",
    )(*args)


def _sc_kernel(x_ref, z_ref, ypart_ref, ea_ref, ch_ref, yp_ref, snw_ref, wout_ref, o_ref, *, nb, nt):
    ch =jnp.concatenate([_load_rows(ch_ref, pl.ds(t, nb, stride=8)) for t in range(nt)], axis=0)
    y = ypart_ref[...] + ea_ref[...] * ch
    yn = _rms(y * _silu(z_ref[...]), snw_ref[...]).astype(BF16)
    out = _dot(jnp.concatenate([yn, yp_ref[...]], axis=1), wout_ref[...])
    o_ref[...] = x_ref[...] + out


def _sc_call(x, z, ypart, ea, ch8, yp, layer, w, nb, nt):
    t_all, d = x.shape
    rows = nt * nb
    xblk = (t_all - rows) // rows

    def full(shape):
        return pl.BlockSpec(shape, lambda i: (0, 0))

    return pl.pallas_call(
        functools.partial(_sc_kernel, nb=nb, nt=nt),
        grid=(1,),
        in_specs=[pl.BlockSpec((rows, d), lambda i: (xblk, 0)), full((rows, D_SSD)), full((rows, D_SSD)),
                  full((rows, D_SSD)), pl.BlockSpec((D_SSD // LANES, nb * 8, LANES), lambda i: (0, 0, 0)),
                  full((rows, POOL_W)),
                  _resident((1, D_SSD), layer), _resident((D_SSD + POOL_W, d), layer)],
        out_specs=pl.BlockSpec((rows, d), lambda i: (0, 0)),
        out_shape=jax.ShapeDtypeStruct((rows, d), F32),
        compiler_params=pltpu.CompilerParams(vmem_limit_bytes=VMEM_LIMIT),
        name="mixer_decode_out",
    )(x, z, ypart, ea, ch8, yp, w["ssd_norm_w"], w["w_out"])


def _prep_weights(w_in, conv_w, conv_b, dt_bias, a_log, d_skip, ssd_norm_w, pool_w, pool_scale, w_out,
                  norm_mix):
    depth = w_in.shape[0]
    o1, o2, o3 = D_SSD, D_SSD + CONV_DIM, D_SSD + CONV_DIM + N_HEADS
    pad_h = ((0, 0), (0, LANES - N_HEADS))
    head_of_lane = jnp.arange(D_SSD) // HEAD_DIM
    e1 = (jnp.arange(LANES)[:, None] == head_of_lane[None, :]).astype(BF16)
    return {
        "norm_mix": norm_mix.reshape(depth, 1, -1),
        "w_in": w_in.astype(BF16),
        "conv_w": conv_w,
        "conv_b": conv_b.reshape(depth, 1, -1),
        "dt_bias": jnp.pad(dt_bias, pad_h).reshape(depth, 1, LANES),
        "a_log": jnp.pad(a_log, pad_h).reshape(depth, 1, LANES),
        "d_skip": jnp.repeat(d_skip, HEAD_DIM, axis=-1).reshape(depth, 1, D_SSD),
        "ssd_norm_w": ssd_norm_w.reshape(depth, 1, -1),
        "pool_w": pool_w.astype(BF16),
        "pool_scale": pool_scale.reshape(depth, 1, -1),
        "w_out": w_out.astype(BF16),
        "e3": jnp.concatenate([e1, e1, e1], axis=0),
    }


def kernel(x_prompt, x_sample, p_prompt, p_sample, state_ssm, state_conv, state_pool, w_in, conv_w, conv_b,
           dt_bias, a_log, d_skip, ssd_norm_w, pool_w, pool_scale, w_out, norm_ffn1, ffn1_gate, ffn1_up,
           ffn1_down, norm_mix, norm_ffn2, ffn2_gate, ffn2_up, ffn2_down, norm_ple, ple_gate, ple_proj,
           final_norm):
    L, sblk = MIX_TILE, DEC_SEQ_BLOCK
    depth = w_in.shape[0]
    batch, seq, d = x_prompt.shape
    nb, nt, _ = x_sample.shape
    n_prompt, n_dec = batch * seq, nt * nb
    tm = n_dec
    n_main = n_prompt // tm
    assert n_main * tm == n_prompt
    mw = _prep_weights(w_in, conv_w, conv_b, dt_bias, a_log, d_skip, ssd_norm_w, pool_w, pool_scale, w_out,
                       norm_mix)
    f1 = (norm_ffn1.reshape(depth, 1, d), ffn1_gate, ffn1_up, ffn1_down)
    f2 = (norm_ffn2.reshape(depth, 1, d), ffn2_gate, ffn2_up, ffn2_down)
    ple = (p_prompt.reshape(depth, n_prompt, -1), jnp.transpose(p_sample, (0, 2, 1, 3)).reshape(depth, n_dec, -1),
           norm_ple.reshape(depth, 1, d), ple_gate.astype(BF16), ple_proj.astype(BF16), final_norm.reshape(1, d))
    x = (x_prompt.reshape(n_prompt, d), jnp.transpose(x_sample, (1, 0, 2)).reshape(n_dec, d))
    h_all = state_ssm.reshape(depth, nb, D_SSD, D_STATE)
    sconv_tm = jnp.transpose(state_conv, (0, 2, 1, 3))
    spool_tm = jnp.transpose(state_pool, (0, 2, 1, 3))

    ssm_p = ssm_s = conv_s = pool_s = None
    conv_p, pool_p = [], []
    for i in range(depth):
        last = i == depth - 1
        x = _ffn_call(x, i, *f1, tm, n_main)
        xm, ssm_p, cp, plp = _mixp_call(x, i, mw, depth, batch, seq, L, ssm_p)
        z, yp, ypart, ea, xw8, cd38, c8, b8, conv_s, pool_s = _sa_call(
            x[n_prompt:].reshape(nt, nb, d), sconv_tm, spool_tm, i, mw, depth, DEC_TOKEN_SEQ_BLOCK, conv_s, pool_s)
        z, yp, ypart, ea = (v.reshape(n_dec, -1) for v in (z, yp, ypart, ea))
        ssm_s, ch8 = _sb_call(h_all, c8, b8, xw8, cd38, i, depth, nb, sblk, ssm_s)
        xd = _sc_call(x, z, ypart, ea, ch8, yp, i, mw, nb, nt)
        x = _ffn_call((xm, xd), i, *f2, tm, n_main, ple=ple, final=last, split_out=last)
        conv_p.append(cp)
        pool_p.append(plp)
    y_prompt = x[0].reshape(batch, seq, d)
    y_sample = jnp.transpose(x[1].reshape(nt, nb, d), (1, 0, 2))
    return (y_prompt, y_sample,
            ssm_p.reshape(depth, batch, N_HEADS, HEAD_DIM, D_STATE), jnp.stack(conv_p), jnp.stack(pool_p),
            ssm_s.reshape(depth, nb, N_HEADS, HEAD_DIM, D_STATE),
            conv_s, pool_s)
```

```python
import functools

import jax
import jax.numpy as jnp
from jax import lax
from jax.experimental import pallas as pl
from jax.experimental.pallas import tpu as pltpu

F32 = jnp.float32
BF16 = jnp.bfloat16

EPS = 1e-6
D_MODEL = 1024
D_SSD = 1024
HEAD_DIM = 64
N_HEADS = 16
N_GROUPS = 2
HEADS_PER_GROUP = 8
D_STATE = 128
CONV_W = 4
CONV_DIM = D_SSD + 2 * N_GROUPS * D_STATE
CHUNK = 128
POOL_WINDOWS = (2, 4, 8, 16)
POOL_GD = 256
POOL_BUF = 15
POOL_W = 1024
LANES = 128
O_Z, O_XBC, O_XP, O_DT, O_END = 0, 1024, 2560, 3584, 3712
VMEM_LIMIT = 56 * 1024 * 1024
D_IN_PROJ = D_SSD + CONV_DIM + N_HEADS + POOL_W
MIX_TILE = 256
DEC_SEQ_BLOCK = 16
DEC_TOKEN_SEQ_BLOCK = 64


def _dot(a, b):
    return jnp.dot(a, b, preferred_element_type=F32)


def _dot_t0(a, b):
    return lax.dot_general(a, b, (((0,), (0,)), ((), ())), preferred_element_type=F32)


def _dot_t1(a, b):
    return lax.dot_general(a, b, (((1,), (1,)), ((), ())), preferred_element_type=F32)


def _rms(x, w):
    return x * lax.rsqrt(jnp.mean(x * x, axis=-1, keepdims=True) + EPS) * w


def _silu(x):
    return x * jax.nn.sigmoid(x)


def _softplus(x):
    return jnp.maximum(x, 0.0) + jnp.log1p(jnp.exp(-jnp.abs(x)))


def _split3(v):
    hi = v.astype(BF16)
    r1 = v - hi.astype(F32)
    mid = r1.astype(BF16)
    lo = (r1 - mid.astype(F32)).astype(BF16)
    return jnp.concatenate([hi, mid, lo], axis=1)


def _expand_heads(v, e3):
    return _dot(_split3(v), e3)


FF_CHUNK = 512


def _ffn_kernel(*refs, layer, ple, final, n_main, split_in, split_out):
    refs = list(refs)
    step = pl.program_id(0)
    is_main = step < n_main
    if split_in:
        xa_ref, xb_ref = refs[:2]
        x = jnp.where(is_main, xa_ref[...], xb_ref[...])
        refs = refs[2:]
    else:
        x = refs[0][...]
        refs = refs[1:]
    nw_ref, wg_hbm, wu_hbm, wd_hbm = refs[:4]
    refs = refs[4:]
    if ple:
        pa_ref, pb_ref, npw_ref, pg_ref, pp_ref, fn_ref = refs[:6]
        refs = refs[6:]
    wg_sc, wu_sc, wd_sc, stage_g, stage_u, stage_d, sem = refs[-7:]
    refs = refs[:-7]
    ff = wg_sc.shape[-1]
    chunks = [(c0, min(c0 + FF_CHUNK, ff)) for c0 in range(0, ff, FF_CHUNK)]

    def copies(c, slot):
        c0, c1 = chunks[c]
        w = c1 - c0
        return (pltpu.make_async_copy(wg_hbm.at[layer, :, c0:c1], stage_g.at[slot, :, 0:w], sem.at[0, slot]),
                pltpu.make_async_copy(wu_hbm.at[layer, :, c0:c1], stage_u.at[slot, :, 0:w], sem.at[1, slot]),
                pltpu.make_async_copy(wd_hbm.at[layer, c0:c1, :], stage_d.at[slot, 0:w, :], sem.at[2, slot]))

    def chunk_product(u, c):
        c0, c1 = chunks[c]
        g = _dot(u, wg_sc[:, c0:c1])
        up = _dot(u, wu_sc[:, c0:c1])
        return _dot((_silu(g) * up).astype(BF16), wd_sc[c0:c1, :])

    def tile(first):
        u = _rms(x, nw_ref[...]).astype(BF16)
        if first:
            for cp in copies(0, 0):
                cp.start()
        acc = None
        for c, (c0, c1) in enumerate(chunks):
            if first:
                slot = c % 2
                if c + 1 < len(chunks):
                    for cp in copies(c + 1, 1 - slot):
                        cp.start()
                for cp in copies(c, slot):
                    cp.wait()
                w = c1 - c0
                wg_sc[:, c0:c1] = stage_g[slot, :, 0:w].astype(BF16)
                wu_sc[:, c0:c1] = stage_u[slot, :, 0:w].astype(BF16)
                wd_sc[c0:c1, :] = stage_d[slot, 0:w, :].astype(BF16)
            d = chunk_product(u, c)
            acc = d if acc is None else acc + d
        x1 = x + 0.5 * acc
        if ple:
            un = _rms(x1, npw_ref[...]).astype(BF16)
            gate = jax.nn.sigmoid(_dot(un, pg_ref[...]))
            p = jnp.where(is_main, pa_ref[...], pb_ref[...]).astype(BF16)
            x1 = x1 + gate * _dot(p, pp_ref[...])
            if final:
                x1 = _rms(x1, fn_ref[...])
        if split_out:
            oa_ref, ob_ref = refs

            @pl.when(is_main)
            def _():
                oa_ref[...] = x1

            @pl.when(jnp.logical_not(is_main))
            def _():
                ob_ref[...] = x1
        else:
            refs[0][...] = x1

    pl.when(step == 0)(functools.partial(tile, True))
    pl.when(step > 0)(functools.partial(tile, False))


def _resident(shape, layer=None):
    n = len(shape)
    if layer is None:
        return pl.BlockSpec(shape, lambda *_: (0,) * n, pipeline_mode=pl.Buffered(1))
    return pl.BlockSpec((None,) + shape, lambda *_: (layer,) + (0,) * n, pipeline_mode=pl.Buffered(1))


def _ffn_call(x, layer, nw, wg, wu, wd, tm, n_main, ple=None, final=False, split_out=False):
    split_in = isinstance(x, tuple)
    d = wg.shape[1]
    ff = wg.shape[-1]

    def main_blk(i):
        return (jnp.minimum(i, n_main - 1), 0)

    def tail_blk(i):
        return (0, 0)

    if split_in:
        assert x[0].shape == (n_main * tm, d) and x[1].shape == (tm, d)
        in_specs = [pl.BlockSpec((tm, d), main_blk), pl.BlockSpec((tm, d), tail_blk)]
        args = list(x)
    else:
        assert x.shape == ((n_main + 1) * tm, d)
        in_specs = [pl.BlockSpec((tm, d), lambda i: (i, 0))]
        args = [x]
    in_specs += [_resident((1, d), layer)] + [pl.BlockSpec(memory_space=pl.ANY)] * 3
    args += [nw, wg, wu, wd]
    if ple is not None:
        pa, pb, npw, pg, pp, fn = ple
        pd = pa.shape[-1]
        in_specs += [
            pl.BlockSpec((None, tm, pd), lambda i: (layer, jnp.minimum(i, n_main - 1), 0)),
            pl.BlockSpec((None, tm, pd), lambda i: (layer, 0, 0)),
            _resident((1, d), layer),
            _resident((d, d), layer),
            _resident((pd, d), layer),
            _resident((1, d)),
        ]
        args += [pa, pb, npw, pg, pp, fn]
    if split_out:
        out_specs = [pl.BlockSpec((tm, d), main_blk), pl.BlockSpec((tm, d), tail_blk)]
        out_shape = [jax.ShapeDtypeStruct((n_main * tm, d), F32), jax.ShapeDtypeStruct((tm, d), F32)]
    else:
        out_specs = pl.BlockSpec((tm, d), lambda i: (i, 0))
        out_shape = jax.ShapeDtypeStruct(((n_main + 1) * tm, d), F32)
    return pl.pallas_call(
        functools.partial(_ffn_kernel, layer=layer, ple=ple is not None, final=final, n_main=n_main,
                          split_in=split_in, split_out=split_out),
        grid=(n_main + 1,),
        in_specs=in_specs,
        out_specs=out_specs,
        out_shape=out_shape,
        scratch_shapes=[
            pltpu.VMEM((d, ff), BF16), pltpu.VMEM((d, ff), BF16), pltpu.VMEM((ff, d), BF16),
            pltpu.VMEM((2, d, FF_CHUNK), F32), pltpu.VMEM((2, d, FF_CHUNK), F32), pltpu.VMEM((2, FF_CHUNK, d), F32),
            pltpu.SemaphoreType.DMA((3, 2))],
        compiler_params=pltpu.CompilerParams(
            dimension_semantics=("arbitrary",), vmem_limit_bytes=VMEM_LIMIT),
        name="ffn_ple" if ple is not None else "ffn",
    )(*args)


PROJ_CHUNK = 256


def _proj_chunks(pieces):
    out, dst = [], 0
    for w_ref, start, width in pieces:
        for lo in range(0, width, PROJ_CHUNK):
            hi = min(lo + PROJ_CHUNK, width)
            out.append((w_ref, start + lo, start + hi, dst + lo))
        dst += width
    return out


def _proj_pieces(w_ref, wxp_sc):
    return ((w_ref, 0, O_XP), (wxp_sc, 0, O_DT - O_XP), (w_ref, O_XP, O_END - O_DT))


def _in_proj(x_ref, nw_ref, w_refs, proj_ref):
    u = _rms(x_ref[...], nw_ref[...]).astype(BF16)
    for w_ref, lo, hi, dst in _proj_chunks(w_refs):
        proj_ref[:, dst:dst + hi - lo] = _dot(u, w_ref[:, lo:hi])


class _Filler:
    def __init__(self, u_ref, w_refs, dst_ref):
        self.u_ref, self.dst_ref = u_ref, dst_ref
        self.todo = _proj_chunks(w_refs)

    def emit(self, n=1):
        for _ in range(n):
            if self.todo:
                w_ref, lo, hi, dst = self.todo.pop(0)
                self.dst_ref[:, dst:dst + hi - lo] = _dot(self.u_ref[...], w_ref[:, lo:hi])

    def drain(self):
        self.emit(len(self.todo))


def _ssd_chunk(c, dt_sc, xbc_sc, y_sc, h_sc, a_row, dsk, e3, tril, causal, lo_mask, filler):
    rows = slice(c * CHUNK, (c + 1) * CHUNK)
    dtc = dt_sc[rows, :]
    acum = _dot(tril, _split3(dtc * a_row))
    acum = acum[:, 0:LANES] + acum[:, LANES:2 * LANES] + acum[:, 2 * LANES:3 * LANES]
    a_t = acum.T[0:N_HEADS, :]
    dt_t = dtc.T[0:N_HEADS, :]
    w_t = dt_t * jnp.exp(a_t[:, CHUNK - 1:CHUNK] - a_t)
    cd = jnp.exp(_expand_heads(acum[CHUNK - 8:CHUNK, :], e3)[7:8, :])
    for g in range(N_GROUPS):
        b_g = xbc_sc[rows, D_SSD + g * D_STATE:D_SSD + (g + 1) * D_STATE]
        c_g = xbc_sc[rows, D_SSD + (N_GROUPS + g) * D_STATE:D_SSD + (N_GROUPS + g + 1) * D_STATE]
        bt_g = b_g.T
        s_g = _dot(c_g.astype(BF16), bt_g.astype(BF16))
        for k in range(HEADS_PER_GROUP // 2):
            lb = g * (HEADS_PER_GROUP // 2) + k
            cols = slice(lb * LANES, (lb + 1) * LANES)
            lhs, btw = [], []
            for r in (2 * lb, 2 * lb + 1):
                lq = acum[:, r:r + 1]
                dec = jnp.where(causal, jnp.exp(lq - a_t[r:r + 1, :]), 0.0)
                lhs.append((s_g * dec * dt_t[r:r + 1, :]).astype(BF16))
                lhs.append((c_g * jnp.exp(lq)).astype(BF16))
                btw.append((bt_g * w_t[r:r + 1, :]).astype(BF16))
            xs = xbc_sc[rows, cols]
            hp = h_sc[:, cols]
            x_lo = jnp.where(lo_mask, xs, 0.0).astype(BF16)
            x_hi = jnp.where(lo_mask, 0.0, xs).astype(BF16)
            h_lo = jnp.where(lo_mask, hp, 0.0).astype(BF16)
            h_hi = jnp.where(lo_mask, 0.0, hp).astype(BF16)
            y = _dot(jnp.concatenate(lhs, axis=1), jnp.concatenate([x_lo, h_lo, x_hi, h_hi], axis=0))
            y_sc[rows, cols] = y + dsk[:, cols] * xs
            upd = _dot(jnp.concatenate(btw, axis=1), jnp.concatenate([x_lo, x_hi], axis=0))
            h_sc[:, cols] = hp * cd[:, cols] + upd
            filler.emit()


def _pool_group_sums(ext, gi, w):
    s = ext[:, gi * POOL_GD:(gi + 1) * POOL_GD]
    sh = 1
    while sh < w:
        s = s + pltpu.roll(s, sh, 0)
        sh *= 2
    return s[16:, :]


def _mixp_kernel(x_ref, xn_ref, nw_ref, w_ref, cw_ref, cb_ref, dtb_ref, alog_ref, dsk_ref,
                 snw_ref, pw_ref, ps_ref, wout_ref, e3_ref,
                 xo_ref, ssm_ref, conv_ref, pool_ref,
                 h_sc, cext_sc, pext_sc, xbc_sc, dt_sc, y_sc, u_sc, proj_a, proj_b, wxp_sc, *, L, n_tiles):
    i = pl.program_id(1)
    g = pl.program_id(0) * n_tiles + i
    w_refs = _proj_pieces(w_ref, wxp_sc)

    @pl.when(g == 0)
    def _():
        wxp_sc[...] = w_ref[:, O_XP + N_HEADS:O_XP + N_HEADS + POOL_W]
        _in_proj(x_ref, nw_ref, w_refs, proj_a)

    @pl.when(i == 0)
    def _():
        h_sc[...] = jnp.zeros_like(h_sc)
        cext_sc[0:8, :] = jnp.zeros((8, CONV_DIM), F32)
        pext_sc[0:16, :] = jnp.zeros((16, POOL_W), F32)

    tile = functools.partial(
        _mixp_tile, i, x_ref, xn_ref, nw_ref, w_refs, cw_ref, cb_ref, dtb_ref, alog_ref, dsk_ref, snw_ref, pw_ref,
        ps_ref, wout_ref, e3_ref, xo_ref, h_sc, cext_sc, pext_sc, xbc_sc, dt_sc, y_sc, u_sc, L=L)

    @pl.when(g % 2 == 0)
    def _():
        tile(proj_a, proj_b)

    @pl.when(g % 2 == 1)
    def _():
        tile(proj_b, proj_a)

    @pl.when(i == n_tiles - 1)
    def _():
        ssm_ref[...] = h_sc[...].T
        conv_ref[...] = cext_sc[L + 5:L + 8, :]
        pool_ref[...] = pext_sc[L + 1:L + 16, :]

    cext_sc[0:8, :] = cext_sc[L:L + 8, :]
    pext_sc[0:16, :] = pext_sc[L:L + 16, :]


def _mixp_tile(i, x_ref, xn_ref, nw_ref, w_refs, cw_ref, cb_ref, dtb_ref, alog_ref, dsk_ref, snw_ref, pw_ref,
               ps_ref, wout_ref, e3_ref, xo_ref, h_sc, cext_sc, pext_sc, xbc_sc, dt_sc, y_sc, u_sc, proj_cur, proj_nxt,
               *, L):
    u_sc[...] = _rms(xn_ref[...], nw_ref[...]).astype(BF16)
    filler = _Filler(u_sc, w_refs, proj_nxt)

    cext_sc[8:8 + L, :] = proj_cur[:, O_XBC:O_XP]
    cw = cw_ref[...]
    cb = cb_ref[...]
    for c0 in range(0, CONV_DIM, PROJ_CHUNK):
        cs = slice(c0, c0 + PROJ_CHUNK)
        conv = cb[:, cs] + cw[3:4, cs] * cext_sc[8:8 + L, cs]
        for k in range(CONV_W - 1):
            conv = conv + cw[k:k + 1, cs] * cext_sc[5 + k:5 + k + L, cs]
        xbc_sc[:, cs] = _silu(conv)
        filler.emit()

    dt_sc[...] = _softplus(proj_cur[:, O_DT:O_END] + dtb_ref[...])
    a_row = -jnp.exp(alog_ref[...])
    filler.emit()

    ri = lax.broadcasted_iota(jnp.int32, (CHUNK, CHUNK), 0)
    ci = lax.broadcasted_iota(jnp.int32, (CHUNK, CHUNK), 1)
    causal = ri >= ci
    tril = causal.astype(BF16)
    lo_mask = ci < HEAD_DIM
    dsk = dsk_ref[...]
    e3 = e3_ref[...]
    xp = proj_cur[:, O_XP:O_DT]
    pext_sc[16:16 + L, :] = xp
    ext = pext_sc[...]
    t_abs = i * L + lax.broadcasted_iota(jnp.int32, (L, 1), 0)
    yps = []
    for gi, w in enumerate(POOL_WINDOWS):
        cnt = jnp.minimum(t_abs + 1, w).astype(F32)
        d = _pool_group_sums(ext, gi, w) / cnt - xp[:, gi * POOL_GD:(gi + 1) * POOL_GD]
        yps.append(_dot(d.astype(BF16), pw_ref[gi]))
    yp = (jnp.concatenate(yps, axis=1) * ps_ref[...]).astype(BF16)
    out_pool = _dot(yp, wout_ref[D_SSD:D_SSD + POOL_W, :])

    for c in range(L // CHUNK):
        _ssd_chunk(c, dt_sc, xbc_sc, y_sc, h_sc, a_row, dsk, e3, tril, causal, lo_mask, filler)
    filler.drain()

    yn = _rms(y_sc[...] * _silu(proj_cur[:, O_Z:O_XBC]), snw_ref[...]).astype(BF16)
    xo_ref[...] = x_ref[...] + (out_pool + _dot(yn, wout_ref[0:D_SSD, :]))


def _mixp_call(x, layer, w, depth, batch, seq, L, ssm_prev):
    n_tiles = seq // L
    d = x.shape[-1]
    last = batch * n_tiles - 1
    in_specs = [
        pl.BlockSpec((L, d), lambda b, i: (b * n_tiles + i, 0)),
        pl.BlockSpec((L, d), lambda b, i: (jnp.minimum(b * n_tiles + i + 1, last), 0)),
        _resident((1, d), layer),
        _resident((d, D_IN_PROJ), layer),
        _resident((CONV_W, CONV_DIM), layer),
        _resident((1, CONV_DIM), layer),
        _resident((1, LANES), layer),
        _resident((1, LANES), layer),
        _resident((1, D_SSD), layer),
        _resident((1, D_SSD), layer),
        _resident((len(POOL_WINDOWS), POOL_GD, POOL_GD), layer),
        _resident((1, POOL_W), layer),
        _resident((D_SSD + POOL_W, d), layer),
        _resident((3 * LANES, D_SSD)),
    ]
    args = [x, x, w["norm_mix"], w["w_in"], w["conv_w"], w["conv_b"], w["dt_bias"],
            w["a_log"], w["d_skip"], w["ssd_norm_w"], w["pool_w"], w["pool_scale"], w["w_out"], w["e3"]]
    n_in = len(args)
    aliases = {}
    if ssm_prev is not None:
        in_specs.append(pl.BlockSpec(memory_space=pl.ANY))
        aliases = {n_in: 1}
        args.append(ssm_prev)
    out_specs = [
        pl.BlockSpec((L, d), lambda b, i: (b * n_tiles + i, 0)),
        pl.BlockSpec((None, None, D_SSD, D_STATE), lambda b, i: (layer, b, 0, 0)),
        pl.BlockSpec((None, CONV_W - 1, CONV_DIM), lambda b, i: (b, 0, 0)),
        pl.BlockSpec((None, POOL_BUF, POOL_W), lambda b, i: (b, 0, 0)),
    ]
    out_shape = [
        jax.ShapeDtypeStruct((batch * seq, d), F32),
        jax.ShapeDtypeStruct((depth, batch, D_SSD, D_STATE), F32),
        jax.ShapeDtypeStruct((batch, CONV_W - 1, CONV_DIM), F32),
        jax.ShapeDtypeStruct((batch, POOL_BUF, POOL_W), F32),
    ]
    scratch = [
        pltpu.VMEM((D_STATE, D_SSD), F32),
        pltpu.VMEM((8 + L, CONV_DIM), F32),
        pltpu.VMEM((16 + L, POOL_W), F32),
        pltpu.VMEM((L, CONV_DIM), F32),
        pltpu.VMEM((L, LANES), F32),
        pltpu.VMEM((L, D_SSD), F32),
        pltpu.VMEM((L, D_MODEL), BF16),
        pltpu.VMEM((L, O_END), F32),
        pltpu.VMEM((L, O_END), F32),
        pltpu.VMEM((D_MODEL, POOL_W), BF16),
    ]

    def body(*refs):
        if ssm_prev is not None:
            refs = refs[:n_in] + refs[n_in + 1:]
        _mixp_kernel(*refs, L=L, n_tiles=n_tiles)

    return pl.pallas_call(
        body,
        grid=(batch, n_tiles),
        in_specs=in_specs,
        out_specs=out_specs,
        out_shape=out_shape,
        scratch_shapes=scratch,
        input_output_aliases=aliases,
        compiler_params=pltpu.CompilerParams(
            dimension_semantics=("arbitrary", "arbitrary"), vmem_limit_bytes=VMEM_LIMIT),
        name="mixer_prompt",
    )(*args)


def _store_seq8(ref, t, val):
    nb = val.shape[0]
    for k in range(ref.shape[0]):
        ref[k, pl.ds(t, nb, stride=8), :] = val[:, k * LANES:(k + 1) * LANES]


def _load_rows(ref, rows):
    return jnp.concatenate([ref[k, rows, :] for k in range(ref.shape[0])], axis=1)


def _sa_kernel(x_ref, sconv_ref, spool_ref, nw_ref, w_ref, cw_ref, cb_ref, dtb_ref, alog_ref,
               dsk_ref, pw_ref, ps_ref, e3_ref,
               z_ref, yp_ref, ypart_ref, ea_ref, xw_ref, cd3_ref, c_ref, b_ref, conv_ref, pool_ref,
               *, nb, nt):
    u = _rms(x_ref[...].reshape(nt * nb, -1), nw_ref[...]).astype(BF16)
    z_ref[...] = _dot(u, w_ref[:, O_Z:O_XBC]).reshape(nt, nb, D_SSD)
    xbc = _dot(u, w_ref[:, O_XBC:O_XP])
    xp = _dot(u, w_ref[:, O_XP + N_HEADS:O_XP + N_HEADS + POOL_W])
    dtr = _dot(u, w_ref[:, O_XP:O_XP + LANES])
    e3 = e3_ref[...]

    def tile(v, t):
        return v[t * nb:(t + 1) * nb, :]

    cext = [sconv_ref[k] for k in range(CONV_W - 1)]
    cext += [tile(xbc, t) for t in range(nt)]
    cw = cw_ref[...]
    xs, bm, cm = [], [], []
    for t in range(nt):
        acc = cb_ref[...]
        for k in range(CONV_W):
            acc = acc + cw[k:k + 1, :] * cext[t + k]
        v = _silu(acc)
        xs.append(v[:, 0:D_SSD])
        bm.append(v[:, D_SSD:D_SSD + N_GROUPS * D_STATE])
        cm.append(v[:, D_SSD + N_GROUPS * D_STATE:])
    for k in range(CONV_W - 1):
        conv_ref[:, k, :] = cext[nt + k]

    dt = _softplus(dtr + dtb_ref[...])
    a_row = -jnp.exp(alog_ref[...])
    dts = [tile(dt, t) for t in range(nt)]
    acum = []
    for t in range(nt):
        da = dts[t] * a_row
        acum.append(da if t == 0 else acum[-1] + da)
    a_last = acum[-1]

    for r in (xw_ref, cd3_ref, c_ref, b_ref):
        r[...] = jnp.zeros(r.shape, F32)
    lane = lax.broadcasted_iota(jnp.int32, (nb, LANES), 1)
    g0_heads = lane < HEADS_PER_GROUP
    dsk = dsk_ref[...]
    for t in range(nt):
        y = dsk * xs[t]
        for s in range(t + 1):
            sc = []
            for g in range(N_GROUPS):
                cg = cm[t][:, g * D_STATE:(g + 1) * D_STATE]
                bg = bm[s][:, g * D_STATE:(g + 1) * D_STATE]
                sc.append(jnp.sum(cg * bg, axis=1, keepdims=True))
            wts = jnp.where(g0_heads, sc[0], sc[1]) * jnp.exp(acum[t] - acum[s]) * dts[s]
            y = y + _expand_heads(wts, e3) * xs[s]
        ypart_ref[t] = y
        ea_ref[t] = _expand_heads(jnp.exp(acum[t]), e3)
        _store_seq8(xw_ref, t, _expand_heads(dts[t] * jnp.exp(a_last - acum[t]), e3) * xs[t])
        _store_seq8(c_ref, t, cm[t])
        _store_seq8(b_ref, t, bm[t])
    cd = jnp.exp(_expand_heads(a_last, e3))
    hi = cd.astype(BF16).astype(F32)
    mid = (cd - hi).astype(BF16).astype(F32)
    lo = (cd - hi - mid).astype(BF16).astype(F32)
    _store_seq8(cd3_ref, 0, hi)
    _store_seq8(cd3_ref, 1, mid)
    _store_seq8(cd3_ref, 2, lo)

    pext = [spool_ref[k] for k in range(POOL_BUF)]
    pext += [tile(xp, t) for t in range(nt)]
    for t in range(nt):
        yps = []
        for gi, w in enumerate(POOL_WINDOWS):
            cols = slice(gi * POOL_GD, (gi + 1) * POOL_GD)
            s = pext[POOL_BUF + t][:, cols]
            for k in range(1, w):
                s = s + pext[POOL_BUF + t - k][:, cols]
            d = s / float(w) - pext[POOL_BUF + t][:, cols]
            yps.append(_dot(d.astype(BF16), pw_ref[gi]))
        yp_ref[t] = (jnp.concatenate(yps, axis=1) * ps_ref[...]).astype(BF16)
    for k in range(POOL_BUF):
        pool_ref[:, k, :] = pext[nt + k]


def _sa_call(x_dec, sconv, spool, layer, w, depth, sb, conv_prev, pool_prev):
    groups, nt, nb, d = x_dec.shape
    in_specs = [
        pl.BlockSpec((None, nt, sb, d), lambda i: (groups - 1, 0, i, 0)),
        pl.BlockSpec((None, CONV_W - 1, sb, CONV_DIM), lambda i: (layer, 0, i, 0)),
        pl.BlockSpec((None, POOL_BUF, sb, POOL_W), lambda i: (layer, 0, i, 0)),
        _resident((1, d), layer),
        _resident((d, D_IN_PROJ), layer),
        _resident((CONV_W, CONV_DIM), layer),
        _resident((1, CONV_DIM), layer),
        _resident((1, LANES), layer),
        _resident((1, LANES), layer),
        _resident((1, D_SSD), layer),
        _resident((len(POOL_WINDOWS), POOL_GD, POOL_GD), layer),
        _resident((1, POOL_W), layer),
        _resident((3 * LANES, D_SSD)),
    ]
    args = [x_dec, sconv, spool, w["norm_mix"], w["w_in"], w["conv_w"], w["conv_b"],
            w["dt_bias"], w["a_log"], w["d_skip"], w["pool_w"], w["pool_scale"], w["e3"]]
    n_in = len(args)
    aliases = {}
    if conv_prev is not None:
        in_specs += [pl.BlockSpec(memory_space=pl.ANY), pl.BlockSpec(memory_space=pl.ANY)]
        aliases = {n_in: 8, n_in + 1: 9}
        args += [conv_prev, pool_prev]
    gn = N_GROUPS * D_STATE
    tm_outs = [(D_SSD, F32), (POOL_W, BF16), (D_SSD, F32), (D_SSD, F32)]
    out_specs = [pl.BlockSpec((nt, sb, wd), lambda i: (0, i, 0)) for wd, _ in tm_outs]
    out_shape = [jax.ShapeDtypeStruct((nt, nb, wd), dt) for wd, dt in tm_outs]
    for wd in (D_SSD, D_SSD, gn, gn):
        out_specs.append(pl.BlockSpec((wd // LANES, sb * 8, LANES), lambda i: (0, i, 0)))
        out_shape.append(jax.ShapeDtypeStruct((wd // LANES, nb * 8, LANES), F32))
    for n_rows, wd in ((CONV_W - 1, CONV_DIM), (POOL_BUF, POOL_W)):
        out_specs.append(pl.BlockSpec((None, sb, n_rows, wd), lambda i: (layer, i, 0, 0)))
        out_shape.append(jax.ShapeDtypeStruct((depth, nb, n_rows, wd), F32))

    def body(*refs):
        if conv_prev is not None:
            refs = refs[:n_in] + refs[n_in + 2:]
        _sa_kernel(*refs, nb=sb, nt=nt)

    return pl.pallas_call(
        body,
        grid=(nb // sb,),
        in_specs=in_specs,
        out_specs=out_specs,
        out_shape=out_shape,
        input_output_aliases=aliases,
        compiler_params=pltpu.CompilerParams(
            dimension_semantics=("arbitrary",), vmem_limit_bytes=VMEM_LIMIT),
        name="mixer_decode_tokens",
    )(*args)


def _sb_kernel(h_ref, c_ref, b_ref, xw_ref, cd3_ref, hn_ref, ch_ref, *, sblk):
    half = D_SSD // N_GROUPS
    ones = jnp.ones((8, D_STATE), BF16)
    for j in range(sblk):
        rows = slice(j * 8, (j + 1) * 8)
        h0 = h_ref[j]
        hb = h0.astype(BF16)
        cj = _load_rows(c_ref, rows).astype(BF16)
        bj = _load_rows(b_ref, rows).astype(BF16)
        xw = _load_rows(xw_ref, rows).astype(BF16)
        dmat = _dot_t0(_load_rows(cd3_ref, rows).astype(BF16), ones)
        chs, upds = [], []
        for g in range(N_GROUPS):
            rs = slice(g * half, (g + 1) * half)
            ns = slice(g * D_STATE, (g + 1) * D_STATE)
            chs.append(_dot_t1(cj[:, ns], hb[rs, :]))
            upds.append(_dot_t0(xw[:, rs], bj[:, ns]))
        ch = jnp.concatenate(chs, axis=1)
        for k in range(ch_ref.shape[0]):
            ch_ref[k, rows, :] = ch[:, k * LANES:(k + 1) * LANES]
        hn_ref[j] = h0 * dmat + jnp.concatenate(upds, axis=0)


def _sb_call(h0, c8, b8, xw8, cd38, layer, depth, nb, sblk, ssm_prev):
    gn = N_GROUPS * D_STATE
    in_specs = [
        pl.BlockSpec((None, sblk, D_SSD, D_STATE), lambda i: (layer, i, 0, 0)),
        pl.BlockSpec((gn // LANES, sblk * 8, LANES), lambda i: (0, i, 0)),
        pl.BlockSpec((gn // LANES, sblk * 8, LANES), lambda i: (0, i, 0)),
        pl.BlockSpec((D_SSD // LANES, sblk * 8, LANES), lambda i: (0, i, 0)),
        pl.BlockSpec((D_SSD // LANES, sblk * 8, LANES), lambda i: (0, i, 0)),
    ]
    args = [h0, c8, b8, xw8, cd38]
    n_in = len(args)
    aliases = {}
    if ssm_prev is not None:
        in_specs.append(pl.BlockSpec(memory_space=pl.ANY))
        aliases = {n_in: 0}
        args.append(ssm_prev)

    def body(*refs):
        if ssm_prev is not None:
            refs = refs[:n_in] + refs[n_in + 1:]
        _sb_kernel(*refs, sblk=sblk)

    return pl.pallas_call(
        body,
        grid=(nb // sblk,),
        in_specs=in_specs,
        out_specs=[
            pl.BlockSpec((None, sblk, D_SSD, D_STATE), lambda i: (layer, i, 0, 0)),
            pl.BlockSpec((D_SSD // LANES, sblk * 8, LANES), lambda i: (0, i, 0)),
        ],
        out_shape=[
            jax.ShapeDtypeStruct((depth, nb, D_SSD, D_STATE), F32),
            jax.ShapeDtypeStruct((D_SSD // LANES, nb * 8, LANES), F32),
        ],
        input_output_aliases=aliases,
        compiler_params=pltpu.CompilerParams(
            dimension_semantics=("parallel",), vmem_limit_bytes=VMEM_LIMIT),
        name="mixer_decode_state",
    )(*args)


def _sc_kernel(x_ref, z_ref, ypart_ref, ea_ref, ch_ref, yp_ref, snw_ref, wout_ref, o_ref, *, nb, nt):
    ch =jnp.concatenate([_load_rows(ch_ref, pl.ds(t, nb, stride=8)) for t in range(nt)], axis=0)
    y = ypart_ref[...] + ea_ref[...] * ch
    yn = _rms(y * _silu(z_ref[...]), snw_ref[...]).astype(BF16)
    out = _dot(jnp.concatenate([yn, yp_ref[...]], axis=1), wout_ref[...])
    o_ref[...] = x_ref[...] + out


def _sc_call(x, z, ypart, ea, ch8, yp, layer, w, nb, nt):
    t_all, d = x.shape
    rows = nt * nb
    xblk = (t_all - rows) // rows

    def full(shape):
        return pl.BlockSpec(shape, lambda i: (0, 0))

    return pl.pallas_call(
        functools.partial(_sc_kernel, nb=nb, nt=nt),
        grid=(1,),
        in_specs=[pl.BlockSpec((rows, d), lambda i: (xblk, 0)), full((rows, D_SSD)), full((rows, D_SSD)),
                  full((rows, D_SSD)), pl.BlockSpec((D_SSD // LANES, nb * 8, LANES), lambda i: (0, 0, 0)),
                  full((rows, POOL_W)),
                  _resident((1, D_SSD), layer), _resident((D_SSD + POOL_W, d), layer)],
        out_specs=pl.BlockSpec((rows, d), lambda i: (0, 0)),
        out_shape=jax.ShapeDtypeStruct((rows, d), F32),
        compiler_params=pltpu.CompilerParams(vmem_limit_bytes=VMEM_LIMIT),
        name="mixer_decode_out",
    )(x, z, ypart, ea, ch8, yp, w["ssd_norm_w"], w["w_out"])


def _prep_weights(w_in, conv_w, conv_b, dt_bias, a_log, d_skip, ssd_norm_w, pool_w, pool_scale, w_out,
                  norm_mix):
    depth = w_in.shape[0]
    pad_h = ((0, 0), (0, LANES - N_HEADS))
    head_of_lane = jnp.arange(D_SSD) // HEAD_DIM
    e1 = (jnp.arange(LANES)[:, None] == head_of_lane[None, :]).astype(BF16)
    return {
        "norm_mix": norm_mix.reshape(depth, 1, -1),
        "w_in": w_in.astype(BF16),
        "conv_w": conv_w,
        "conv_b": conv_b.reshape(depth, 1, -1),
        "dt_bias": jnp.pad(dt_bias, pad_h).reshape(depth, 1, LANES),
        "a_log": jnp.pad(a_log, pad_h).reshape(depth, 1, LANES),
        "d_skip": jnp.repeat(d_skip, HEAD_DIM, axis=-1).reshape(depth, 1, D_SSD),
        "ssd_norm_w": ssd_norm_w.reshape(depth, 1, -1),
        "pool_w": pool_w.astype(BF16),
        "pool_scale": pool_scale.reshape(depth, 1, -1),
        "w_out": w_out.astype(BF16),
        "e3": jnp.concatenate([e1, e1, e1], axis=0),
    }


def kernel(x_prompt, x_sample, p_prompt, p_sample, state_ssm, state_conv, state_pool, w_in, conv_w, conv_b,
           dt_bias, a_log, d_skip, ssd_norm_w, pool_w, pool_scale, w_out, norm_ffn1, ffn1_gate, ffn1_up,
           ffn1_down, norm_mix, norm_ffn2, ffn2_gate, ffn2_up, ffn2_down, norm_ple, ple_gate, ple_proj,
           final_norm):
    L, sblk = MIX_TILE, DEC_SEQ_BLOCK
    depth = w_in.shape[0]
    batch, seq, d = x_prompt.shape
    nb, nt, _ = x_sample.shape
    n_prompt, n_dec = batch * seq, nt * nb
    tm = n_dec
    n_main = n_prompt // tm
    assert n_main * tm == n_prompt
    mw = _prep_weights(w_in, conv_w, conv_b, dt_bias, a_log, d_skip, ssd_norm_w, pool_w, pool_scale, w_out,
                       norm_mix)
    f1 = (norm_ffn1.reshape(depth, 1, d), ffn1_gate, ffn1_up, ffn1_down)
    f2 = (norm_ffn2.reshape(depth, 1, d), ffn2_gate, ffn2_up, ffn2_down)
    ple = (p_prompt.reshape(depth, n_prompt, -1), jnp.transpose(p_sample, (0, 2, 1, 3)).reshape(depth, n_dec, -1),
           norm_ple.reshape(depth, 1, d), ple_gate.astype(BF16), ple_proj.astype(BF16), final_norm.reshape(1, d))
    x = (x_prompt.reshape(n_prompt, d), jnp.transpose(x_sample, (1, 0, 2)).reshape(n_dec, d))
    h_all = state_ssm.reshape(depth, nb, D_SSD, D_STATE)
    sconv_tm = jnp.transpose(state_conv, (0, 2, 1, 3))
    spool_tm = jnp.transpose(state_pool, (0, 2, 1, 3))

    ssm_p = ssm_s = conv_s = pool_s = None
    conv_p, pool_p = [], []
    for i in range(depth):
        last = i == depth - 1
        x = _ffn_call(x, i, *f1, tm, n_main)
        xm, ssm_p, cp, plp = _mixp_call(x, i, mw, depth, batch, seq, L, ssm_p)
        z, yp, ypart, ea, xw8, cd38, c8, b8, conv_s, pool_s = _sa_call(
            x.reshape(n_main + 1, nt, nb, d), sconv_tm, spool_tm, i, mw, depth, DEC_TOKEN_SEQ_BLOCK, conv_s, pool_s)
        z, yp, ypart, ea = (v.reshape(n_dec, -1) for v in (z, yp, ypart, ea))
        ssm_s, ch8 = _sb_call(h_all, c8, b8, xw8, cd38, i, depth, nb, sblk, ssm_s)
        xd = _sc_call(x, z, ypart, ea, ch8, yp, i, mw, nb, nt)
        x = _ffn_call((xm, xd), i, *f2, tm, n_main, ple=ple, final=last, split_out=last)
        conv_p.append(cp)
        pool_p.append(plp)
    y_prompt = x[0].reshape(batch, seq, d)
    y_sample = jnp.transpose(x[1].reshape(nt, nb, d), (1, 0, 2))
    return (y_prompt, y_sample,
            ssm_p.reshape(depth, batch, N_HEADS, HEAD_DIM, D_STATE), jnp.stack(conv_p), jnp.stack(pool_p),
            ssm_s.reshape(depth, nb, N_HEADS, HEAD_DIM, D_STATE),
            conv_s, pool_s)
```

```python
import functools

import jax
import jax.numpy as jnp
from jax import lax
from jax.experimental import pallas as pl
from jax.experimental.pallas import tpu as pltpu

F32 = jnp.float32
BF16 = jnp.bfloat16

EPS = 1e-6
D_MODEL = 1024
D_SSD = 1024
HEAD_DIM = 64
N_HEADS = 16
N_GROUPS = 2
HEADS_PER_GROUP = 8
D_STATE = 128
CONV_W = 4
CONV_DIM = D_SSD + 2 * N_GROUPS * D_STATE
CHUNK = 128
POOL_WINDOWS = (2, 4, 8, 16)
POOL_GD = 256
POOL_BUF = 15
POOL_W = 1024
LANES = 128
O_Z, O_XBC, O_XP, O_DT, O_END = 0, 1024, 2560, 3584, 3712
VMEM_LIMIT = 56 * 1024 * 1024
D_IN_PROJ = D_SSD + CONV_DIM + N_HEADS + POOL_W
MIX_TILE = 256
DEC_SEQ_BLOCK = 16
DEC_TOKEN_SEQ_BLOCK = 64


def _dot(a, b):
    return jnp.dot(a, b, preferred_element_type=F32)


def _dot_t0(a, b):
    return lax.dot_general(a, b, (((0,), (0,)), ((), ())), preferred_element_type=F32)


def _dot_t1(a, b):
    return lax.dot_general(a, b, (((1,), (1,)), ((), ())), preferred_element_type=F32)


def _rms(x, w):
    return x * lax.rsqrt(jnp.mean(x * x, axis=-1, keepdims=True) + EPS) * w


def _silu(x):
    return x * jax.nn.sigmoid(x)


def _softplus(x):
    return jnp.maximum(x, 0.0) + jnp.log1p(jnp.exp(-jnp.abs(x)))


def _split3(v):
    hi = v.astype(BF16)
    r1 = v - hi.astype(F32)
    mid = r1.astype(BF16)
    lo = (r1 - mid.astype(F32)).astype(BF16)
    return jnp.concatenate([hi, mid, lo], axis=1)


def _expand_heads(v, e3):
    return _dot(_split3(v), e3)


FF_CHUNK = 256


def _ffn_kernel(*refs, layer, ple, final, n_main, split_in, split_out):
    refs = list(refs)
    step = pl.program_id(0)
    is_main = step < n_main
    if split_in:
        xa_ref, xb_ref = refs[:2]
        x = jnp.where(is_main, xa_ref[...], xb_ref[...])
        refs = refs[2:]
    else:
        x = refs[0][...]
        refs = refs[1:]
    nw_ref, wg_hbm, wu_hbm, wd_hbm = refs[:4]
    refs = refs[4:]
    if ple:
        pa_ref, pb_ref, npw_ref, pg_ref, pp_ref, fn_ref = refs[:6]
        refs = refs[6:]
    wg_sc, wu_sc, wd_sc, stage_g, stage_u, stage_d, sem = refs[-7:]
    refs = refs[:-7]
    ff = wg_sc.shape[-1]
    chunks = [(c0, min(c0 + FF_CHUNK, ff)) for c0 in range(0, ff, FF_CHUNK)]

    def copies(c, slot):
        c0, c1 = chunks[c]
        w = c1 - c0
        return (pltpu.make_async_copy(wg_hbm.at[layer, :, c0:c1], stage_g.at[slot, :, 0:w], sem.at[0, slot]),
                pltpu.make_async_copy(wu_hbm.at[layer, :, c0:c1], stage_u.at[slot, :, 0:w], sem.at[1, slot]),
                pltpu.make_async_copy(wd_hbm.at[layer, c0:c1, :], stage_d.at[slot, 0:w, :], sem.at[2, slot]))

    def chunk_product(u, c):
        c0, c1 = chunks[c]
        g = _dot(u, wg_sc[:, c0:c1])
        up = _dot(u, wu_sc[:, c0:c1])
        return _dot((_silu(g) * up).astype(BF16), wd_sc[c0:c1, :])

    def tile(first):
        u = _rms(x, nw_ref[...]).astype(BF16)
        if first:
            for cp in copies(0, 0):
                cp.start()
        acc = None
        for c, (c0, c1) in enumerate(chunks):
            if first:
                slot = c % 2
                if c + 1 < len(chunks):
                    for cp in copies(c + 1, 1 - slot):
                        cp.start()
                for cp in copies(c, slot):
                    cp.wait()
                w = c1 - c0
                wg_sc[:, c0:c1] = stage_g[slot, :, 0:w].astype(BF16)
                wu_sc[:, c0:c1] = stage_u[slot, :, 0:w].astype(BF16)
                wd_sc[c0:c1, :] = stage_d[slot, 0:w, :].astype(BF16)
            d = chunk_product(u, c)
            acc = d if acc is None else acc + d
        x1 = x + 0.5 * acc
        if ple:
            un = _rms(x1, npw_ref[...]).astype(BF16)
            gate = jax.nn.sigmoid(_dot(un, pg_ref[...]))
            p = jnp.where(is_main, pa_ref[...], pb_ref[...]).astype(BF16)
            x1 = x1 + gate * _dot(p, pp_ref[...])
            if final:
                x1 = _rms(x1, fn_ref[...])
        if split_out:
            oa_ref, ob_ref = refs

            @pl.when(is_main)
            def _():
                oa_ref[...] = x1

            @pl.when(jnp.logical_not(is_main))
            def _():
                ob_ref[...] = x1
        else:
            refs[0][...] = x1

    pl.when(step == 0)(functools.partial(tile, True))
    pl.when(step > 0)(functools.partial(tile, False))


def _resident(shape, layer=None):
    n = len(shape)
    if layer is None:
        return pl.BlockSpec(shape, lambda *_: (0,) * n, pipeline_mode=pl.Buffered(1))
    return pl.BlockSpec((None,) + shape, lambda *_: (layer,) + (0,) * n, pipeline_mode=pl.Buffered(1))


def _ffn_call(x, layer, nw, wg, wu, wd, tm, n_main, ple=None, final=False, split_out=False):
    split_in = isinstance(x, tuple)
    d = wg.shape[1]
    ff = wg.shape[-1]

    def main_blk(i):
        return (jnp.minimum(i, n_main - 1), 0)

    def tail_blk(i):
        return (0, 0)

    if split_in:
        assert x[0].shape == (n_main * tm, d) and x[1].shape == (tm, d)
        in_specs = [pl.BlockSpec((tm, d), main_blk), pl.BlockSpec((tm, d), tail_blk)]
        args = list(x)
    else:
        assert x.shape == ((n_main + 1) * tm, d)
        in_specs = [pl.BlockSpec((tm, d), lambda i: (i, 0))]
        args = [x]
    in_specs += [_resident((1, d), layer)] + [pl.BlockSpec(memory_space=pl.ANY)] * 3
    args += [nw, wg, wu, wd]
    if ple is not None:
        pa, pb, npw, pg, pp, fn = ple
        pd = pa.shape[-1]
        in_specs += [
            pl.BlockSpec((None, tm, pd), lambda i: (layer, jnp.minimum(i, n_main - 1), 0)),
            pl.BlockSpec((None, tm, pd), lambda i: (layer, 0, 0)),
            _resident((1, d), layer),
            _resident((d, d), layer),
            _resident((pd, d), layer),
            _resident((1, d)),
        ]
        args += [pa, pb, npw, pg, pp, fn]
    if split_out:
        out_specs = [pl.BlockSpec((tm, d), main_blk), pl.BlockSpec((tm, d), tail_blk)]
        out_shape = [jax.ShapeDtypeStruct((n_main * tm, d), F32), jax.ShapeDtypeStruct((tm, d), F32)]
    else:
        out_specs = pl.BlockSpec((tm, d), lambda i: (i, 0))
        out_shape = jax.ShapeDtypeStruct(((n_main + 1) * tm, d), F32)
    return pl.pallas_call(
        functools.partial(_ffn_kernel, layer=layer, ple=ple is not None, final=final, n_main=n_main,
                          split_in=split_in, split_out=split_out),
        grid=(n_main + 1,),
        in_specs=in_specs,
        out_specs=out_specs,
        out_shape=out_shape,
        scratch_shapes=[
            pltpu.VMEM((d, ff), BF16), pltpu.VMEM((d, ff), BF16), pltpu.VMEM((ff, d), BF16),
            pltpu.VMEM((2, d, FF_CHUNK), F32), pltpu.VMEM((2, d, FF_CHUNK), F32), pltpu.VMEM((2, FF_CHUNK, d), F32),
            pltpu.SemaphoreType.DMA((3, 2))],
        compiler_params=pltpu.CompilerParams(
            dimension_semantics=("arbitrary",), vmem_limit_bytes=VMEM_LIMIT),
        name="ffn_ple" if ple is not None else "ffn",
    )(*args)


PROJ_CHUNK = 256


def _proj_chunks(pieces):
    out, dst = [], 0
    for w_ref, start, width in pieces:
        for lo in range(0, width, PROJ_CHUNK):
            hi = min(lo + PROJ_CHUNK, width)
            out.append((w_ref, start + lo, start + hi, dst + lo))
        dst += width
    return out


def _proj_pieces(w_ref, wxp_sc):
    return ((w_ref, 0, O_XP), (wxp_sc, 0, O_DT - O_XP), (w_ref, O_XP, O_END - O_DT))


def _in_proj(x_ref, nw_ref, w_refs, proj_ref):
    u = _rms(x_ref[...], nw_ref[...]).astype(BF16)
    for w_ref, lo, hi, dst in _proj_chunks(w_refs):
        proj_ref[:, dst:dst + hi - lo] = _dot(u, w_ref[:, lo:hi])


class _Filler:
    def __init__(self, u_ref, w_refs, dst_ref):
        self.u_ref, self.dst_ref = u_ref, dst_ref
        self.todo = _proj_chunks(w_refs)

    def emit(self, n=1):
        for _ in range(n):
            if self.todo:
                w_ref, lo, hi, dst = self.todo.pop(0)
                self.dst_ref[:, dst:dst + hi - lo] = _dot(self.u_ref[...], w_ref[:, lo:hi])

    def drain(self):
        self.emit(len(self.todo))


def _ssd_chunk(c, dt_sc, xbc_sc, y_sc, h_sc, a_row, dsk, e3, tril, causal, lo_mask, filler):
    rows = slice(c * CHUNK, (c + 1) * CHUNK)
    dtc = dt_sc[rows, :]
    acum = _dot(tril, _split3(dtc * a_row))
    acum = acum[:, 0:LANES] + acum[:, LANES:2 * LANES] + acum[:, 2 * LANES:3 * LANES]
    a_t = acum.T[0:N_HEADS, :]
    dt_t = dtc.T[0:N_HEADS, :]
    w_t = dt_t * jnp.exp(a_t[:, CHUNK - 1:CHUNK] - a_t)
    cd = jnp.exp(_expand_heads(acum[CHUNK - 8:CHUNK, :], e3)[7:8, :])
    for g in range(N_GROUPS):
        b_g = xbc_sc[rows, D_SSD + g * D_STATE:D_SSD + (g + 1) * D_STATE]
        c_g = xbc_sc[rows, D_SSD + (N_GROUPS + g) * D_STATE:D_SSD + (N_GROUPS + g + 1) * D_STATE]
        bt_g = b_g.T
        s_g = _dot(c_g.astype(BF16), bt_g.astype(BF16))
        for k in range(HEADS_PER_GROUP // 2):
            lb = g * (HEADS_PER_GROUP // 2) + k
            cols = slice(lb * LANES, (lb + 1) * LANES)
            lhs, btw = [], []
            for r in (2 * lb, 2 * lb + 1):
                lq = acum[:, r:r + 1]
                dec = jnp.where(causal, jnp.exp(lq - a_t[r:r + 1, :]), 0.0)
                lhs.append((s_g * dec * dt_t[r:r + 1, :]).astype(BF16))
                lhs.append((c_g * jnp.exp(lq)).astype(BF16))
                btw.append((bt_g * w_t[r:r + 1, :]).astype(BF16))
            xs = xbc_sc[rows, cols]
            hp = h_sc[:, cols]
            x_lo = jnp.where(lo_mask, xs, 0.0).astype(BF16)
            x_hi = jnp.where(lo_mask, 0.0, xs).astype(BF16)
            h_lo = jnp.where(lo_mask, hp, 0.0).astype(BF16)
            h_hi = jnp.where(lo_mask, 0.0, hp).astype(BF16)
            y = _dot(jnp.concatenate(lhs, axis=1), jnp.concatenate([x_lo, h_lo, x_hi, h_hi], axis=0))
            y_sc[rows, cols] = y + dsk[:, cols] * xs
            upd = _dot(jnp.concatenate(btw, axis=1), jnp.concatenate([x_lo, x_hi], axis=0))
            h_sc[:, cols] = hp * cd[:, cols] + upd
            filler.emit()


def _pool_group_sums(ext, gi, w):
    s = ext[:, gi * POOL_GD:(gi + 1) * POOL_GD]
    sh = 1
    while sh < w:
        s = s + pltpu.roll(s, sh, 0)
        sh *= 2
    return s[16:, :]


def _mixp_kernel(x_ref, xn_ref, nw_ref, w_ref, cw_ref, cb_ref, dtb_ref, alog_ref, dsk_ref,
                 snw_ref, pw_ref, ps_ref, wout_ref, e3_ref,
                 xo_ref, ssm_ref, conv_ref, pool_ref,
                 h_sc, cext_sc, pext_sc, xbc_sc, dt_sc, y_sc, u_sc, proj_a, proj_b, wxp_sc, *, L, n_tiles):
    i = pl.program_id(1)
    g = pl.program_id(0) * n_tiles + i
    w_refs = _proj_pieces(w_ref, wxp_sc)

    @pl.when(g == 0)
    def _():
        wxp_sc[...] = w_ref[:, O_XP + N_HEADS:O_XP + N_HEADS + POOL_W]
        _in_proj(x_ref, nw_ref, w_refs, proj_a)

    @pl.when(i == 0)
    def _():
        h_sc[...] = jnp.zeros_like(h_sc)
        cext_sc[0:8, :] = jnp.zeros((8, CONV_DIM), F32)
        pext_sc[0:16, :] = jnp.zeros((16, POOL_W), F32)

    tile = functools.partial(
        _mixp_tile, i, x_ref, xn_ref, nw_ref, w_refs, cw_ref, cb_ref, dtb_ref, alog_ref, dsk_ref, snw_ref, pw_ref,
        ps_ref, wout_ref, e3_ref, xo_ref, h_sc, cext_sc, pext_sc, xbc_sc, dt_sc, y_sc, u_sc, L=L)

    @pl.when(g % 2 == 0)
    def _():
        tile(proj_a, proj_b)

    @pl.when(g % 2 == 1)
    def _():
        tile(proj_b, proj_a)

    @pl.when(i == n_tiles - 1)
    def _():
        ssm_ref[...] = h_sc[...].T
        conv_ref[...] = cext_sc[L + 5:L + 8, :]
        pool_ref[...] = pext_sc[L + 1:L + 16, :]

    cext_sc[0:8, :] = cext_sc[L:L + 8, :]
    pext_sc[0:16, :] = pext_sc[L:L + 16, :]


def _mixp_tile(i, x_ref, xn_ref, nw_ref, w_refs, cw_ref, cb_ref, dtb_ref, alog_ref, dsk_ref, snw_ref, pw_ref,
               ps_ref, wout_ref, e3_ref, xo_ref, h_sc, cext_sc, pext_sc, xbc_sc, dt_sc, y_sc, u_sc, proj_cur, proj_nxt,
               *, L):
    u_sc[...] = _rms(xn_ref[...], nw_ref[...]).astype(BF16)
    filler = _Filler(u_sc, w_refs, proj_nxt)

    cext_sc[8:8 + L, :] = proj_cur[:, O_XBC:O_XP]
    cw = cw_ref[...]
    cb = cb_ref[...]
    for c0 in range(0, CONV_DIM, PROJ_CHUNK):
        cs = slice(c0, c0 + PROJ_CHUNK)
        conv = cb[:, cs] + cw[3:4, cs] * cext_sc[8:8 + L, cs]
        for k in range(CONV_W - 1):
            conv = conv + cw[k:k + 1, cs] * cext_sc[5 + k:5 + k + L, cs]
        xbc_sc[:, cs] = _silu(conv)
        filler.emit()

    dt_sc[...] = _softplus(proj_cur[:, O_DT:O_END] + dtb_ref[...])
    a_row = -jnp.exp(alog_ref[...])
    filler.emit()

    ri = lax.broadcasted_iota(jnp.int32, (CHUNK, CHUNK), 0)
    ci = lax.broadcasted_iota(jnp.int32, (CHUNK, CHUNK), 1)
    causal = ri >= ci
    tril = causal.astype(BF16)
    lo_mask = ci < HEAD_DIM
    dsk = dsk_ref[...]
    e3 = e3_ref[...]
    xp = proj_cur[:, O_XP:O_DT]
    pext_sc[16:16 + L, :] = xp
    ext = pext_sc[...]
    t_abs = i * L + lax.broadcasted_iota(jnp.int32, (L, 1), 0)
    yps = []
    for gi, w in enumerate(POOL_WINDOWS):
        cnt = jnp.minimum(t_abs + 1, w).astype(F32)
        d = _pool_group_sums(ext, gi, w) / cnt - xp[:, gi * POOL_GD:(gi + 1) * POOL_GD]
        yps.append(_dot(d.astype(BF16), pw_ref[gi]))
    yp = (jnp.concatenate(yps, axis=1) * ps_ref[...]).astype(BF16)
    out_pool = _dot(yp, wout_ref[D_SSD:D_SSD + POOL_W, :])

    for c in range(L // CHUNK):
        _ssd_chunk(c, dt_sc, xbc_sc, y_sc, h_sc, a_row, dsk, e3, tril, causal, lo_mask, filler)
    filler.drain()

    yn = _rms(y_sc[...] * _silu(proj_cur[:, O_Z:O_XBC]), snw_ref[...]).astype(BF16)
    xo_ref[...] = x_ref[...] + (out_pool + _dot(yn, wout_ref[0:D_SSD, :]))


def _mixp_call(x, layer, w, depth, batch, seq, L, ssm_prev):
    n_tiles = seq // L
    d = x.shape[-1]
    last = batch * n_tiles - 1
    in_specs = [
        pl.BlockSpec((L, d), lambda b, i: (b * n_tiles + i, 0)),
        pl.BlockSpec((L, d), lambda b, i: (jnp.minimum(b * n_tiles + i + 1, last), 0)),
        _resident((1, d), layer),
        _resident((d, D_IN_PROJ), layer),
        _resident((CONV_W, CONV_DIM), layer),
        _resident((1, CONV_DIM), layer),
        _resident((1, LANES), layer),
        _resident((1, LANES), layer),
        _resident((1, D_SSD), layer),
        _resident((1, D_SSD), layer),
        _resident((len(POOL_WINDOWS), POOL_GD, POOL_GD), layer),
        _resident((1, POOL_W), layer),
        _resident((D_SSD + POOL_W, d), layer),
        _resident((3 * LANES, D_SSD)),
    ]
    args = [x, x, w["norm_mix"], w["w_in"], w["conv_w"], w["conv_b"], w["dt_bias"],
            w["a_log"], w["d_skip"], w["ssd_norm_w"], w["pool_w"], w["pool_scale"], w["w_out"], w["e3"]]
    n_in = len(args)
    aliases = {}
    if ssm_prev is not None:
        in_specs.append(pl.BlockSpec(memory_space=pl.ANY))
        aliases = {n_in: 1}
        args.append(ssm_prev)
    out_specs = [
        pl.BlockSpec((L, d), lambda b, i: (b * n_tiles + i, 0)),
        pl.BlockSpec((None, None, D_SSD, D_STATE), lambda b, i: (layer, b, 0, 0)),
        pl.BlockSpec((None, CONV_W - 1, CONV_DIM), lambda b, i: (b, 0, 0)),
        pl.BlockSpec((None, POOL_BUF, POOL_W), lambda b, i: (b, 0, 0)),
    ]
    out_shape = [
        jax.ShapeDtypeStruct((batch * seq, d), F32),
        jax.ShapeDtypeStruct((depth, batch, D_SSD, D_STATE), F32),
        jax.ShapeDtypeStruct((batch, CONV_W - 1, CONV_DIM), F32),
        jax.ShapeDtypeStruct((batch, POOL_BUF, POOL_W), F32),
    ]
    scratch = [
        pltpu.VMEM((D_STATE, D_SSD), F32),
        pltpu.VMEM((8 + L, CONV_DIM), F32),
        pltpu.VMEM((16 + L, POOL_W), F32),
        pltpu.VMEM((L, CONV_DIM), F32),
        pltpu.VMEM((L, LANES), F32),
        pltpu.VMEM((L, D_SSD), F32),
        pltpu.VMEM((L, D_MODEL), BF16),
        pltpu.VMEM((L, O_END), F32),
        pltpu.VMEM((L, O_END), F32),
        pltpu.VMEM((D_MODEL, POOL_W), BF16),
    ]

    def body(*refs):
        if ssm_prev is not None:
            refs = refs[:n_in] + refs[n_in + 1:]
        _mixp_kernel(*refs, L=L, n_tiles=n_tiles)

    return pl.pallas_call(
        body,
        grid=(batch, n_tiles),
        in_specs=in_specs,
        out_specs=out_specs,
        out_shape=out_shape,
        scratch_shapes=scratch,
        input_output_aliases=aliases,
        compiler_params=pltpu.CompilerParams(
            dimension_semantics=("arbitrary", "arbitrary"), vmem_limit_bytes=VMEM_LIMIT),
        name="mixer_prompt",
    )(*args)


def _store_seq8(ref, t, val):
    nb = val.shape[0]
    for k in range(ref.shape[0]):
        ref[k, pl.ds(t, nb, stride=8), :] = val[:, k * LANES:(k + 1) * LANES]


def _load_rows(ref, rows):
    return jnp.concatenate([ref[k, rows, :] for k in range(ref.shape[0])], axis=1)


def _sa_kernel(x_ref, sconv_ref, spool_ref, nw_ref, w_ref, cw_ref, cb_ref, dtb_ref, alog_ref,
               dsk_ref, pw_ref, ps_ref, e3_ref,
               z_ref, yp_ref, ypart_ref, ea_ref, xw_ref, cd3_ref, c_ref, b_ref, conv_ref, pool_ref,
               *, nb, nt):
    u = _rms(x_ref[...].reshape(nt * nb, -1), nw_ref[...]).astype(BF16)
    z_ref[...] = _dot(u, w_ref[:, O_Z:O_XBC]).reshape(nt, nb, D_SSD)
    xbc = _dot(u, w_ref[:, O_XBC:O_XP])
    xp = _dot(u, w_ref[:, O_XP + N_HEADS:O_XP + N_HEADS + POOL_W])
    dtr = _dot(u, w_ref[:, O_XP:O_XP + LANES])
    e3 = e3_ref[...]

    def tile(v, t):
        return v[t * nb:(t + 1) * nb, :]

    cext = [sconv_ref[k] for k in range(CONV_W - 1)]
    cext += [tile(xbc, t) for t in range(nt)]
    cw = cw_ref[...]
    xs, bm, cm = [], [], []
    for t in range(nt):
        acc = cb_ref[...]
        for k in range(CONV_W):
            acc = acc + cw[k:k + 1, :] * cext[t + k]
        v = _silu(acc)
        xs.append(v[:, 0:D_SSD])
        bm.append(v[:, D_SSD:D_SSD + N_GROUPS * D_STATE])
        cm.append(v[:, D_SSD + N_GROUPS * D_STATE:])
    for k in range(CONV_W - 1):
        conv_ref[:, k, :] = cext[nt + k]

    dt = _softplus(dtr + dtb_ref[...])
    a_row = -jnp.exp(alog_ref[...])
    dts = [tile(dt, t) for t in range(nt)]
    acum = []
    for t in range(nt):
        da = dts[t] * a_row
        acum.append(da if t == 0 else acum[-1] + da)
    a_last = acum[-1]

    for r in (xw_ref, cd3_ref, c_ref, b_ref):
        r[...] = jnp.zeros(r.shape, F32)
    lane = lax.broadcasted_iota(jnp.int32, (nb, LANES), 1)
    g0_heads = lane < HEADS_PER_GROUP
    dsk = dsk_ref[...]
    for t in range(nt):
        y = dsk * xs[t]
        for s in range(t + 1):
            sc = []
            for g in range(N_GROUPS):
                cg = cm[t][:, g * D_STATE:(g + 1) * D_STATE]
                bg = bm[s][:, g * D_STATE:(g + 1) * D_STATE]
                sc.append(jnp.sum(cg * bg, axis=1, keepdims=True))
            wts = jnp.where(g0_heads, sc[0], sc[1]) * jnp.exp(acum[t] - acum[s]) * dts[s]
            y = y + _expand_heads(wts, e3) * xs[s]
        ypart_ref[t] = y
        ea_ref[t] = _expand_heads(jnp.exp(acum[t]), e3)
        _store_seq8(xw_ref, t, _expand_heads(dts[t] * jnp.exp(a_last - acum[t]), e3) * xs[t])
        _store_seq8(c_ref, t, cm[t])
        _store_seq8(b_ref, t, bm[t])
    cd = jnp.exp(_expand_heads(a_last, e3))
    hi = cd.astype(BF16).astype(F32)
    mid = (cd - hi).astype(BF16).astype(F32)
    lo = (cd - hi - mid).astype(BF16).astype(F32)
    _store_seq8(cd3_ref, 0, hi)
    _store_seq8(cd3_ref, 1, mid)
    _store_seq8(cd3_ref, 2, lo)

    pext = [spool_ref[k] for k in range(POOL_BUF)]
    pext += [tile(xp, t) for t in range(nt)]
    for t in range(nt):
        yps = []
        for gi, w in enumerate(POOL_WINDOWS):
            cols = slice(gi * POOL_GD, (gi + 1) * POOL_GD)
            s = pext[POOL_BUF + t][:, cols]
            for k in range(1, w):
                s = s + pext[POOL_BUF + t - k][:, cols]
            d = s / float(w) - pext[POOL_BUF + t][:, cols]
            yps.append(_dot(d.astype(BF16), pw_ref[gi]))
        yp_ref[t] = (jnp.concatenate(yps, axis=1) * ps_ref[...]).astype(BF16)
    for k in range(POOL_BUF):
        pool_ref[:, k, :] = pext[nt + k]


def _sa_call(x_dec, sconv, spool, layer, w, depth, sb, conv_prev, pool_prev):
    groups, nt, nb, d = x_dec.shape
    in_specs = [
        pl.BlockSpec((None, nt, sb, d), lambda i: (groups - 1, 0, i, 0)),
        pl.BlockSpec((None, CONV_W - 1, sb, CONV_DIM), lambda i: (layer, 0, i, 0)),
        pl.BlockSpec((None, POOL_BUF, sb, POOL_W), lambda i: (layer, 0, i, 0)),
        _resident((1, d), layer),
        _resident((d, D_IN_PROJ), layer),
        _resident((CONV_W, CONV_DIM), layer),
        _resident((1, CONV_DIM), layer),
        _resident((1, LANES), layer),
        _resident((1, LANES), layer),
        _resident((1, D_SSD), layer),
        _resident((len(POOL_WINDOWS), POOL_GD, POOL_GD), layer),
        _resident((1, POOL_W), layer),
        _resident((3 * LANES, D_SSD)),
    ]
    args = [x_dec, sconv, spool, w["norm_mix"], w["w_in"], w["conv_w"], w["conv_b"],
            w["dt_bias"], w["a_log"], w["d_skip"], w["pool_w"], w["pool_scale"], w["e3"]]
    n_in = len(args)
    aliases = {}
    if conv_prev is not None:
        in_specs += [pl.BlockSpec(memory_space=pl.ANY), pl.BlockSpec(memory_space=pl.ANY)]
        aliases = {n_in: 8, n_in + 1: 9}
        args += [conv_prev, pool_prev]
    gn = N_GROUPS * D_STATE
    tm_outs = [(D_SSD, F32), (POOL_W, BF16), (D_SSD, F32), (D_SSD, F32)]
    out_specs = [pl.BlockSpec((nt, sb, wd), lambda i: (0, i, 0)) for wd, _ in tm_outs]
    out_shape = [jax.ShapeDtypeStruct((nt, nb, wd), dt) for wd, dt in tm_outs]
    for wd in (D_SSD, D_SSD, gn, gn):
        out_specs.append(pl.BlockSpec((wd // LANES, sb * 8, LANES), lambda i: (0, i, 0)))
        out_shape.append(jax.ShapeDtypeStruct((wd // LANES, nb * 8, LANES), F32))
    for n_rows, wd in ((CONV_W - 1, CONV_DIM), (POOL_BUF, POOL_W)):
        out_specs.append(pl.BlockSpec((None, sb, n_rows, wd), lambda i: (layer, i, 0, 0)))
        out_shape.append(jax.ShapeDtypeStruct((depth, nb, n_rows, wd), F32))

    def body(*refs):
        if conv_prev is not None:
            refs = refs[:n_in] + refs[n_in + 2:]
        _sa_kernel(*refs, nb=sb, nt=nt)

    return pl.pallas_call(
        body,
        grid=(nb // sb,),
        in_specs=in_specs,
        out_specs=out_specs,
        out_shape=out_shape,
        input_output_aliases=aliases,
        compiler_params=pltpu.CompilerParams(
            dimension_semantics=("arbitrary",), vmem_limit_bytes=VMEM_LIMIT),
        name="mixer_decode_tokens",
    )(*args)


def _sb_kernel(h_ref, c_ref, b_ref, xw_ref, cd3_ref, hn_ref, ch_ref, *, sblk):
    half = D_SSD // N_GROUPS
    ones = jnp.ones((8, D_STATE), BF16)
    for j in range(sblk):
        rows = slice(j * 8, (j + 1) * 8)
        h0 = h_ref[j]
        hb = h0.astype(BF16)
        cj = _load_rows(c_ref, rows).astype(BF16)
        bj = _load_rows(b_ref, rows).astype(BF16)
        xw = _load_rows(xw_ref, rows).astype(BF16)
        dmat = _dot_t0(_load_rows(cd3_ref, rows).astype(BF16), ones)
        chs, upds = [], []
        for g in range(N_GROUPS):
            rs = slice(g * half, (g + 1) * half)
            ns = slice(g * D_STATE, (g + 1) * D_STATE)
            chs.append(_dot_t1(cj[:, ns], hb[rs, :]))
            upds.append(_dot_t0(xw[:, rs], bj[:, ns]))
        ch = jnp.concatenate(chs, axis=1)
        for k in range(ch_ref.shape[0]):
            ch_ref[k, rows, :] = ch[:, k * LANES:(k + 1) * LANES]
        hn_ref[j] = h0 * dmat + jnp.concatenate(upds, axis=0)


def _sb_call(h0, c8, b8, xw8, cd38, layer, depth, nb, sblk, ssm_prev):
    gn = N_GROUPS * D_STATE
    in_specs = [
        pl.BlockSpec((None, sblk, D_SSD, D_STATE), lambda i: (layer, i, 0, 0)),
        pl.BlockSpec((gn // LANES, sblk * 8, LANES), lambda i: (0, i, 0)),
        pl.BlockSpec((gn // LANES, sblk * 8, LANES), lambda i: (0, i, 0)),
        pl.BlockSpec((D_SSD // LANES, sblk * 8, LANES), lambda i: (0, i, 0)),
        pl.BlockSpec((D_SSD // LANES, sblk * 8, LANES), lambda i: (0, i, 0)),
    ]
    args = [h0, c8, b8, xw8, cd38]
    n_in = len(args)
    aliases = {}
    if ssm_prev is not None:
        in_specs.append(pl.BlockSpec(memory_space=pl.ANY))
        aliases = {n_in: 0}
        args.append(ssm_prev)

    def body(*refs):
        if ssm_prev is not None:
            refs = refs[:n_in] + refs[n_in + 1:]
        _sb_kernel(*refs, sblk=sblk)

    return pl.pallas_call(
        body,
        grid=(nb // sblk,),
        in_specs=in_specs,
        out_specs=[
            pl.BlockSpec((None, sblk, D_SSD, D_STATE), lambda i: (layer, i, 0, 0)),
            pl.BlockSpec((D_SSD // LANES, sblk * 8, LANES), lambda i: (0, i, 0)),
        ],
        out_shape=[
            jax.ShapeDtypeStruct((depth, nb, D_SSD, D_STATE), F32),
            jax.ShapeDtypeStruct((D_SSD // LANES, nb * 8, LANES), F32),
        ],
        input_output_aliases=aliases,
        compiler_params=pltpu.CompilerParams(
            dimension_semantics=("parallel",), vmem_limit_bytes=VMEM_LIMIT),
        name="mixer_decode_state",
    )(*args)


def _sc_kernel(x_ref, z_ref, ypart_ref, ea_ref, ch_ref, yp_ref, snw_ref, wout_ref, o_ref, *, nb, nt):
    ch =jnp.concatenate([_load_rows(ch_ref, pl.ds(t, nb, stride=8)) for t in range(nt)], axis=0)
    y = ypart_ref[...] + ea_ref[...] * ch
    yn = _rms(y * _silu(z_ref[...]), snw_ref[...]).astype(BF16)
    out = _dot(jnp.concatenate([yn, yp_ref[...]], axis=1), wout_ref[...])
    o_ref[...] = x_ref[...] + out


def _sc_call(x, z, ypart, ea, ch8, yp, layer, w, nb, nt):
    t_all, d = x.shape
    rows = nt * nb
    xblk = (t_all - rows) // rows

    def full(shape):
        return pl.BlockSpec(shape, lambda i: (0, 0))

    return pl.pallas_call(
        functools.partial(_sc_kernel, nb=nb, nt=nt),
        grid=(1,),
        in_specs=[pl.BlockSpec((rows, d), lambda i: (xblk, 0)), full((rows, D_SSD)), full((rows, D_SSD)),
                  full((rows, D_SSD)), pl.BlockSpec((D_SSD // LANES, nb * 8, LANES), lambda i: (0, 0, 0)),
                  full((rows, POOL_W)),
                  _resident((1, D_SSD), layer), _resident((D_SSD + POOL_W, d), layer)],
        out_specs=pl.BlockSpec((rows, d), lambda i: (0, 0)),
        out_shape=jax.ShapeDtypeStruct((rows, d), F32),
        compiler_params=pltpu.CompilerParams(vmem_limit_bytes=VMEM_LIMIT),
        name="mixer_decode_out",
    )(x, z, ypart, ea, ch8, yp, w["ssd_norm_w"], w["w_out"])


def _prep_weights(w_in, conv_w, conv_b, dt_bias, a_log, d_skip, ssd_norm_w, pool_w, pool_scale, w_out,
                  norm_mix):
    depth = w_in.shape[0]
    pad_h = ((0, 0), (0, LANES - N_HEADS))
    head_of_lane = jnp.arange(D_SSD) // HEAD_DIM
    e1 = (jnp.arange(LANES)[:, None] == head_of_lane[None, :]).astype(BF16)
    return {
        "norm_mix": norm_mix.reshape(depth, 1, -1),
        "w_in": w_in.astype(BF16),
        "conv_w": conv_w,
        "conv_b": conv_b.reshape(depth, 1, -1),
        "dt_bias": jnp.pad(dt_bias, pad_h).reshape(depth, 1, LANES),
        "a_log": jnp.pad(a_log, pad_h).reshape(depth, 1, LANES),
        "d_skip": jnp.repeat(d_skip, HEAD_DIM, axis=-1).reshape(depth, 1, D_SSD),
        "ssd_norm_w": ssd_norm_w.reshape(depth, 1, -1),
        "pool_w": pool_w.astype(BF16),
        "pool_scale": pool_scale.reshape(depth, 1, -1),
        "w_out": w_out.astype(BF16),
        "e3": jnp.concatenate([e1, e1, e1], axis=0),
    }


def kernel(x_prompt, x_sample, p_prompt, p_sample, state_ssm, state_conv, state_pool, w_in, conv_w, conv_b,
           dt_bias, a_log, d_skip, ssd_norm_w, pool_w, pool_scale, w_out, norm_ffn1, ffn1_gate, ffn1_up,
           ffn1_down, norm_mix, norm_ffn2, ffn2_gate, ffn2_up, ffn2_down, norm_ple, ple_gate, ple_proj,
           final_norm):
    L, sblk = MIX_TILE, DEC_SEQ_BLOCK
    depth = w_in.shape[0]
    batch, seq, d = x_prompt.shape
    nb, nt, _ = x_sample.shape
    n_prompt, n_dec = batch * seq, nt * nb
    tm = n_dec
    n_main = n_prompt // tm
    assert n_main * tm == n_prompt
    mw = _prep_weights(w_in, conv_w, conv_b, dt_bias, a_log, d_skip, ssd_norm_w, pool_w, pool_scale, w_out,
                       norm_mix)
    f1 = (norm_ffn1.reshape(depth, 1, d), ffn1_gate, ffn1_up, ffn1_down)
    f2 = (norm_ffn2.reshape(depth, 1, d), ffn2_gate, ffn2_up, ffn2_down)
    ple = (p_prompt.reshape(depth, n_prompt, -1), jnp.transpose(p_sample, (0, 2, 1, 3)).reshape(depth, n_dec, -1),
           norm_ple.reshape(depth, 1, d), ple_gate.astype(BF16), ple_proj.astype(BF16), final_norm.reshape(1, d))
    x = (x_prompt.reshape(n_prompt, d), jnp.transpose(x_sample, (1, 0, 2)).reshape(n_dec, d))
    h_all = state_ssm.reshape(depth, nb, D_SSD, D_STATE)
    sconv_tm = jnp.transpose(state_conv, (0, 2, 1, 3))
    spool_tm = jnp.transpose(state_pool, (0, 2, 1, 3))

    ssm_p = ssm_s = conv_s = pool_s = None
    conv_p, pool_p = [], []
    for i in range(depth):
        last = i == depth - 1
        x = _ffn_call(x, i, *f1, tm, n_main)
        xm, ssm_p, cp, plp = _mixp_call(x, i, mw, depth, batch, seq, L, ssm_p)
        z, yp, ypart, ea, xw8, cd38, c8, b8, conv_s, pool_s = _sa_call(
            x.reshape(n_main + 1, nt, nb, d), sconv_tm, spool_tm, i, mw, depth, DEC_TOKEN_SEQ_BLOCK, conv_s, pool_s)
        z, yp, ypart, ea = (v.reshape(n_dec, -1) for v in (z, yp, ypart, ea))
        ssm_s, ch8 = _sb_call(h_all, c8, b8, xw8, cd38, i, depth, nb, sblk, ssm_s)
        xd = _sc_call(x, z, ypart, ea, ch8, yp, i, mw, nb, nt)
        x = _ffn_call((xm, xd), i, *f2, tm, n_main, ple=ple, final=last, split_out=last)
        conv_p.append(cp)
        pool_p.append(plp)
    y_prompt = x[0].reshape(batch, seq, d)
    y_sample = jnp.transpose(x[1].reshape(nt, nb, d), (1, 0, 2))
    return (y_prompt, y_sample,
            ssm_p.reshape(depth, batch, N_HEADS, HEAD_DIM, D_STATE), jnp.stack(conv_p), jnp.stack(pool_p),
            ssm_s.reshape(depth, nb, N_HEADS, HEAD_DIM, D_STATE),
            conv_s, pool_s)
```

```python
import functools

import jax
import jax.numpy as jnp
from jax import lax
from jax.experimental import pallas as pl
from jax.experimental.pallas import tpu as pltpu

F32 = jnp.float32
BF16 = jnp.bfloat16

EPS = 1e-6
D_MODEL = 1024
D_SSD = 1024
HEAD_DIM = 64
N_HEADS = 16
N_GROUPS = 2
HEADS_PER_GROUP = 8
D_STATE = 128
CONV_W = 4
CONV_DIM = D_SSD + 2 * N_GROUPS * D_STATE
CHUNK = 128
POOL_WINDOWS = (2, 4, 8, 16)
POOL_GD = 256
POOL_BUF = 15
POOL_W = 1024
LANES = 128
O_Z, O_XBC, O_XP, O_DT, O_END = 0, 1024, 2560, 3584, 3712
VMEM_LIMIT = 56 * 1024 * 1024
D_IN_PROJ = D_SSD + CONV_DIM + N_HEADS + POOL_W
MIX_TILE = 256
DEC_SEQ_BLOCK = 16
DEC_TOKEN_SEQ_BLOCK = 64


def _dot(a, b):
    return jnp.dot(a, b, preferred_element_type=F32)


def _dot_t0(a, b):
    return lax.dot_general(a, b, (((0,), (0,)), ((), ())), preferred_element_type=F32)


def _dot_t1(a, b):
    return lax.dot_general(a, b, (((1,), (1,)), ((), ())), preferred_element_type=F32)


def _rms(x, w):
    return x * lax.rsqrt(jnp.mean(x * x, axis=-1, keepdims=True) + EPS) * w


def _silu(x):
    return x * jax.nn.sigmoid(x)


def _softplus(x):
    return jnp.maximum(x, 0.0) + jnp.log1p(jnp.exp(-jnp.abs(x)))


def _split3(v):
    hi = v.astype(BF16)
    r1 = v - hi.astype(F32)
    mid = r1.astype(BF16)
    lo = (r1 - mid.astype(F32)).astype(BF16)
    return jnp.concatenate([hi, mid, lo], axis=1)


def _expand_heads(v, e3):
    return _dot(_split3(v), e3)


FF_CHUNK = 256


def _ffn_kernel(*refs, layer, ple, final, n_main, split_in, split_out):
    refs = list(refs)
    step = pl.program_id(0)
    is_main = step < n_main
    if split_in:
        xa_ref, xb_ref = refs[:2]
        x = jnp.where(is_main, xa_ref[...], xb_ref[...])
        refs = refs[2:]
    else:
        x = refs[0][...]
        refs = refs[1:]
    nw_ref, wg_hbm, wu_hbm, wd_hbm = refs[:4]
    refs = refs[4:]
    if ple:
        pa_ref, pb_ref, npw_ref, pg_ref, pp_ref, fn_ref = refs[:6]
        refs = refs[6:]
    wg_sc, wu_sc, wd_sc, stage_g, stage_u, stage_d, sem = refs[-7:]
    refs = refs[:-7]
    ff = wg_sc.shape[-1]
    chunks = [(c0, min(c0 + FF_CHUNK, ff)) for c0 in range(0, ff, FF_CHUNK)]

    def copies(c, slot):
        c0, c1 = chunks[c]
        w = c1 - c0
        return (pltpu.make_async_copy(wg_hbm.at[layer, :, c0:c1], stage_g.at[slot, :, 0:w], sem.at[0, slot]),
                pltpu.make_async_copy(wu_hbm.at[layer, :, c0:c1], stage_u.at[slot, :, 0:w], sem.at[1, slot]),
                pltpu.make_async_copy(wd_hbm.at[layer, c0:c1, :], stage_d.at[slot, 0:w, :], sem.at[2, slot]))

    def chunk_product(u, c):
        c0, c1 = chunks[c]
        g = _dot(u, wg_sc[:, c0:c1])
        up = _dot(u, wu_sc[:, c0:c1])
        return _dot((_silu(g) * up).astype(BF16), wd_sc[c0:c1, :])

    def tile(first):
        u = _rms(x, nw_ref[...]).astype(BF16)
        if first:
            for cp in copies(0, 0):
                cp.start()
        acc = None
        for c, (c0, c1) in enumerate(chunks):
            if first:
                slot = c % 2
                if c + 1 < len(chunks):
                    for cp in copies(c + 1, 1 - slot):
                        cp.start()
                for cp in copies(c, slot):
                    cp.wait()
                w = c1 - c0
                wg_sc[:, c0:c1] = stage_g[slot, :, 0:w].astype(BF16)
                wu_sc[:, c0:c1] = stage_u[slot, :, 0:w].astype(BF16)
                wd_sc[c0:c1, :] = stage_d[slot, 0:w, :].astype(BF16)
            d = chunk_product(u, c)
            acc = d if acc is None else acc + d
        x1 = x + 0.5 * acc
        if ple:
            un = _rms(x1, npw_ref[...]).astype(BF16)
            gate = jax.nn.sigmoid(_dot(un, pg_ref[...]))
            p = jnp.where(is_main, pa_ref[...], pb_ref[...]).astype(BF16)
            x1 = x1 + gate * _dot(p, pp_ref[...])
            if final:
                x1 = _rms(x1, fn_ref[...])
        if split_out:
            oa_ref, ob_ref = refs

            @pl.when(is_main)
            def _():
                oa_ref[...] = x1

            @pl.when(jnp.logical_not(is_main))
            def _():
                ob_ref[...] = x1
        else:
            refs[0][...] = x1

    pl.when(step == 0)(functools.partial(tile, True))
    pl.when(step > 0)(functools.partial(tile, False))


def _resident(shape, layer=None):
    n = len(shape)
    if layer is None:
        return pl.BlockSpec(shape, lambda *_: (0,) * n, pipeline_mode=pl.Buffered(1))
    return pl.BlockSpec((None,) + shape, lambda *_: (layer,) + (0,) * n, pipeline_mode=pl.Buffered(1))


def _ffn_call(x, layer, nw, wg, wu, wd, tm, n_main, ple=None, final=False, split_out=False):
    split_in = isinstance(x, tuple)
    d = wg.shape[1]
    ff = wg.shape[-1]

    def main_blk(i):
        return (jnp.minimum(i, n_main - 1), 0)

    def tail_blk(i):
        return (0, 0)

    if split_in:
        assert x[0].shape == (n_main * tm, d) and x[1].shape == (tm, d)
        in_specs = [pl.BlockSpec((tm, d), main_blk), pl.BlockSpec((tm, d), tail_blk)]
        args = list(x)
    else:
        assert x.shape == ((n_main + 1) * tm, d)
        in_specs = [pl.BlockSpec((tm, d), lambda i: (i, 0))]
        args = [x]
    in_specs += [_resident((1, d), layer)] + [pl.BlockSpec(memory_space=pl.ANY)] * 3
    args += [nw, wg, wu, wd]
    if ple is not None:
        pa, pb, npw, pg, pp, fn = ple
        pd = pa.shape[-1]
        in_specs += [
            pl.BlockSpec((None, tm, pd), lambda i: (layer, jnp.minimum(i, n_main - 1), 0)),
            pl.BlockSpec((None, tm, pd), lambda i: (layer, 0, 0)),
            _resident((1, d), layer),
            _resident((d, d), layer),
            _resident((pd, d), layer),
            _resident((1, d)),
        ]
        args += [pa, pb, npw, pg, pp, fn]
    if split_out:
        out_specs = [pl.BlockSpec((tm, d), main_blk), pl.BlockSpec((tm, d), tail_blk)]
        out_shape = [jax.ShapeDtypeStruct((n_main * tm, d), F32), jax.ShapeDtypeStruct((tm, d), F32)]
    else:
        out_specs = pl.BlockSpec((tm, d), lambda i: (i, 0))
        out_shape = jax.ShapeDtypeStruct(((n_main + 1) * tm, d), F32)
    return pl.pallas_call(
        functools.partial(_ffn_kernel, layer=layer, ple=ple is not None, final=final, n_main=n_main,
                          split_in=split_in, split_out=split_out),
        grid=(n_main + 1,),
        in_specs=in_specs,
        out_specs=out_specs,
        out_shape=out_shape,
        scratch_shapes=[
            pltpu.VMEM((d, ff), BF16), pltpu.VMEM((d, ff), BF16), pltpu.VMEM((ff, d), BF16),
            pltpu.VMEM((2, d, FF_CHUNK), F32), pltpu.VMEM((2, d, FF_CHUNK), F32), pltpu.VMEM((2, FF_CHUNK, d), F32),
            pltpu.SemaphoreType.DMA((3, 2))],
        compiler_params=pltpu.CompilerParams(
            dimension_semantics=("arbitrary",), vmem_limit_bytes=VMEM_LIMIT),
        name="ffn_ple" if ple is not None else "ffn",
    )(*args)


PROJ_CHUNK = 256
CONV_BLOCK = 768


def _proj_chunks(pieces):
    out, dst = [], 0
    for w_ref, start, width in pieces:
        for lo in range(0, width, PROJ_CHUNK):
            hi = min(lo + PROJ_CHUNK, width)
            out.append((w_ref, start + lo, start + hi, dst + lo))
        dst += width
    return out


def _proj_pieces(w_ref, wxp_sc):
    return ((w_ref, 0, O_XP), (wxp_sc, 0, O_DT - O_XP), (w_ref, O_XP, O_END - O_DT))


def _in_proj(x_ref, nw_ref, w_refs, proj_ref):
    u = _rms(x_ref[...], nw_ref[...]).astype(BF16)
    for w_ref, lo, hi, dst in _proj_chunks(w_refs):
        proj_ref[:, dst:dst + hi - lo] = _dot(u, w_ref[:, lo:hi])


class _Filler:
    def __init__(self, u_ref, w_refs, dst_ref):
        self.u_ref, self.dst_ref = u_ref, dst_ref
        self.todo = _proj_chunks(w_refs)

    def emit(self, n=1):
        for _ in range(n):
            if self.todo:
                w_ref, lo, hi, dst = self.todo.pop(0)
                self.dst_ref[:, dst:dst + hi - lo] = _dot(self.u_ref[...], w_ref[:, lo:hi])

    def drain(self):
        self.emit(len(self.todo))


def _ssd_chunk(c, dt_sc, xbc_sc, y_sc, h_sc, a_row, dsk, e3, tril, causal, lo_mask, filler):
    rows = slice(c * CHUNK, (c + 1) * CHUNK)
    dtc = dt_sc[rows, :]
    acum = _dot(tril, _split3(dtc * a_row))
    acum = acum[:, 0:LANES] + acum[:, LANES:2 * LANES] + acum[:, 2 * LANES:3 * LANES]
    a_t = acum.T[0:N_HEADS, :]
    dt_t = dtc.T[0:N_HEADS, :]
    w_t = dt_t * jnp.exp(a_t[:, CHUNK - 1:CHUNK] - a_t)
    cd = jnp.exp(_expand_heads(acum[CHUNK - 8:CHUNK, :], e3)[7:8, :])
    for g in range(N_GROUPS):
        b_g = xbc_sc[rows, D_SSD + g * D_STATE:D_SSD + (g + 1) * D_STATE]
        c_g = xbc_sc[rows, D_SSD + (N_GROUPS + g) * D_STATE:D_SSD + (N_GROUPS + g + 1) * D_STATE]
        bt_g = b_g.T
        s_g = _dot(c_g.astype(BF16), bt_g.astype(BF16))
        for k in range(HEADS_PER_GROUP // 2):
            lb = g * (HEADS_PER_GROUP // 2) + k
            cols = slice(lb * LANES, (lb + 1) * LANES)
            lhs, btw = [], []
            for r in (2 * lb, 2 * lb + 1):
                lq = acum[:, r:r + 1]
                dec = jnp.where(causal, jnp.exp(lq - a_t[r:r + 1, :]), 0.0)
                lhs.append((s_g * dec * dt_t[r:r + 1, :]).astype(BF16))
                lhs.append((c_g * jnp.exp(lq)).astype(BF16))
                btw.append((bt_g * w_t[r:r + 1, :]).astype(BF16))
            xs = xbc_sc[rows, cols]
            hp = h_sc[:, cols]
            x_lo = jnp.where(lo_mask, xs, 0.0).astype(BF16)
            x_hi = jnp.where(lo_mask, 0.0, xs).astype(BF16)
            h_lo = jnp.where(lo_mask, hp, 0.0).astype(BF16)
            h_hi = jnp.where(lo_mask, 0.0, hp).astype(BF16)
            y = _dot(jnp.concatenate(lhs, axis=1), jnp.concatenate([x_lo, h_lo, x_hi, h_hi], axis=0))
            y_sc[rows, cols] = y + dsk[:, cols] * xs
            upd = _dot(jnp.concatenate(btw, axis=1), jnp.concatenate([x_lo, x_hi], axis=0))
            h_sc[:, cols] = hp * cd[:, cols] + upd
            filler.emit()


def _pool_group_sums(ext, gi, w):
    s = ext[:, gi * POOL_GD:(gi + 1) * POOL_GD]
    sh = 1
    while sh < w:
        s = s + pltpu.roll(s, sh, 0)
        sh *= 2
    return s[16:, :]


def _mixp_kernel(x_ref, xn_ref, nw_ref, w_ref, cw_ref, cb_ref, dtb_ref, alog_ref, dsk_ref,
                 snw_ref, pw_ref, ps_ref, wout_ref, e3_ref,
                 xo_ref, ssm_ref, conv_ref, pool_ref,
                 h_sc, cext_sc, pext_sc, xbc_sc, dt_sc, y_sc, u_sc, proj_a, proj_b, wxp_sc, *, L, n_tiles):
    i = pl.program_id(1)
    g = pl.program_id(0) * n_tiles + i
    w_refs = _proj_pieces(w_ref, wxp_sc)

    @pl.when(g == 0)
    def _():
        wxp_sc[...] = w_ref[:, O_XP + N_HEADS:O_XP + N_HEADS + POOL_W]
        _in_proj(x_ref, nw_ref, w_refs, proj_a)

    @pl.when(i == 0)
    def _():
        h_sc[...] = jnp.zeros_like(h_sc)
        cext_sc[0:8, :] = jnp.zeros((8, CONV_DIM), F32)
        pext_sc[0:16, :] = jnp.zeros((16, POOL_W), F32)

    tile = functools.partial(
        _mixp_tile, i, x_ref, xn_ref, nw_ref, w_refs, cw_ref, cb_ref, dtb_ref, alog_ref, dsk_ref, snw_ref, pw_ref,
        ps_ref, wout_ref, e3_ref, xo_ref, h_sc, cext_sc, pext_sc, xbc_sc, dt_sc, y_sc, u_sc, L=L)

    @pl.when(g % 2 == 0)
    def _():
        tile(proj_a, proj_b)

    @pl.when(g % 2 == 1)
    def _():
        tile(proj_b, proj_a)

    @pl.when(i == n_tiles - 1)
    def _():
        ssm_ref[...] = h_sc[...].T
        conv_ref[...] = cext_sc[L + 5:L + 8, :]
        pool_ref[...] = pext_sc[L + 1:L + 16, :]

    cext_sc[0:8, :] = cext_sc[L:L + 8, :]
    pext_sc[0:16, :] = pext_sc[L:L + 16, :]


def _mixp_tile(i, x_ref, xn_ref, nw_ref, w_refs, cw_ref, cb_ref, dtb_ref, alog_ref, dsk_ref, snw_ref, pw_ref,
               ps_ref, wout_ref, e3_ref, xo_ref, h_sc, cext_sc, pext_sc, xbc_sc, dt_sc, y_sc, u_sc, proj_cur, proj_nxt,
               *, L):
    u_sc[...] = _rms(xn_ref[...], nw_ref[...]).astype(BF16)
    filler = _Filler(u_sc, w_refs, proj_nxt)

    cext_sc[8:8 + L, :] = proj_cur[:, O_XBC:O_XP]
    cw = cw_ref[...]
    cb = cb_ref[...]
    for c0 in range(0, CONV_DIM, CONV_BLOCK):
        cs = slice(c0, c0 + CONV_BLOCK)
        conv = cb[:, cs] + cw[3:4, cs] * cext_sc[8:8 + L, cs]
        for k in range(CONV_W - 1):
            conv = conv + cw[k:k + 1, cs] * cext_sc[5 + k:5 + k + L, cs]
        xbc_sc[:, cs] = _silu(conv)
        filler.emit()

    dt_sc[...] = _softplus(proj_cur[:, O_DT:O_END] + dtb_ref[...])
    a_row = -jnp.exp(alog_ref[...])
    filler.emit()

    ri = lax.broadcasted_iota(jnp.int32, (CHUNK, CHUNK), 0)
    ci = lax.broadcasted_iota(jnp.int32, (CHUNK, CHUNK), 1)
    causal = ri >= ci
    tril = causal.astype(BF16)
    lo_mask = ci < HEAD_DIM
    dsk = dsk_ref[...]
    e3 = e3_ref[...]
    xp = proj_cur[:, O_XP:O_DT]
    pext_sc[16:16 + L, :] = xp
    ext = pext_sc[...]
    t_abs = i * L + lax.broadcasted_iota(jnp.int32, (L, 1), 0)
    yps = []
    for gi, w in enumerate(POOL_WINDOWS):
        cnt = jnp.minimum(t_abs + 1, w).astype(F32)
        d = _pool_group_sums(ext, gi, w) / cnt - xp[:, gi * POOL_GD:(gi + 1) * POOL_GD]
        yps.append(_dot(d.astype(BF16), pw_ref[gi]))
    yp = (jnp.concatenate(yps, axis=1) * ps_ref[...]).astype(BF16)
    out_pool = _dot(yp, wout_ref[D_SSD:D_SSD + POOL_W, :])

    for c in range(L // CHUNK):
        _ssd_chunk(c, dt_sc, xbc_sc, y_sc, h_sc, a_row, dsk, e3, tril, causal, lo_mask, filler)
    filler.drain()

    yn = _rms(y_sc[...] * _silu(proj_cur[:, O_Z:O_XBC]), snw_ref[...]).astype(BF16)
    xo_ref[...] = x_ref[...] + (out_pool + _dot(yn, wout_ref[0:D_SSD, :]))


def _mixp_call(x, layer, w, depth, batch, seq, L, ssm_prev):
    n_tiles = seq // L
    d = x.shape[-1]
    last = batch * n_tiles - 1
    in_specs = [
        pl.BlockSpec((L, d), lambda b, i: (b * n_tiles + i, 0)),
        pl.BlockSpec((L, d), lambda b, i: (jnp.minimum(b * n_tiles + i + 1, last), 0)),
        _resident((1, d), layer),
        _resident((d, D_IN_PROJ), layer),
        _resident((CONV_W, CONV_DIM), layer),
        _resident((1, CONV_DIM), layer),
        _resident((1, LANES), layer),
        _resident((1, LANES), layer),
        _resident((1, D_SSD), layer),
        _resident((1, D_SSD), layer),
        _resident((len(POOL_WINDOWS), POOL_GD, POOL_GD), layer),
        _resident((1, POOL_W), layer),
        _resident((D_SSD + POOL_W, d), layer),
        _resident((3 * LANES, D_SSD)),
    ]
    args = [x, x, w["norm_mix"], w["w_in"], w["conv_w"], w["conv_b"], w["dt_bias"],
            w["a_log"], w["d_skip"], w["ssd_norm_w"], w["pool_w"], w["pool_scale"], w["w_out"], w["e3"]]
    n_in = len(args)
    aliases = {}
    if ssm_prev is not None:
        in_specs.append(pl.BlockSpec(memory_space=pl.ANY))
        aliases = {n_in: 1}
        args.append(ssm_prev)
    out_specs = [
        pl.BlockSpec((L, d), lambda b, i: (b * n_tiles + i, 0)),
        pl.BlockSpec((None, None, D_SSD, D_STATE), lambda b, i: (layer, b, 0, 0)),
        pl.BlockSpec((None, CONV_W - 1, CONV_DIM), lambda b, i: (b, 0, 0)),
        pl.BlockSpec((None, POOL_BUF, POOL_W), lambda b, i: (b, 0, 0)),
    ]
    out_shape = [
        jax.ShapeDtypeStruct((batch * seq, d), F32),
        jax.ShapeDtypeStruct((depth, batch, D_SSD, D_STATE), F32),
        jax.ShapeDtypeStruct((batch, CONV_W - 1, CONV_DIM), F32),
        jax.ShapeDtypeStruct((batch, POOL_BUF, POOL_W), F32),
    ]
    scratch = [
        pltpu.VMEM((D_STATE, D_SSD), F32),
        pltpu.VMEM((8 + L, CONV_DIM), F32),
        pltpu.VMEM((16 + L, POOL_W), F32),
        pltpu.VMEM((L, CONV_DIM), F32),
        pltpu.VMEM((L, LANES), F32),
        pltpu.VMEM((L, D_SSD), F32),
        pltpu.VMEM((L, D_MODEL), BF16),
        pltpu.VMEM((L, O_END), F32),
        pltpu.VMEM((L, O_END), F32),
        pltpu.VMEM((D_MODEL, POOL_W), BF16),
    ]

    def body(*refs):
        if ssm_prev is not None:
            refs = refs[:n_in] + refs[n_in + 1:]
        _mixp_kernel(*refs, L=L, n_tiles=n_tiles)

    return pl.pallas_call(
        body,
        grid=(batch, n_tiles),
        in_specs=in_specs,
        out_specs=out_specs,
        out_shape=out_shape,
        scratch_shapes=scratch,
        input_output_aliases=aliases,
        compiler_params=pltpu.CompilerParams(
            dimension_semantics=("arbitrary", "arbitrary"), vmem_limit_bytes=VMEM_LIMIT),
        name="mixer_prompt",
    )(*args)


def _store_seq8(ref, t, val):
    nb = val.shape[0]
    for k in range(ref.shape[0]):
        ref[k, pl.ds(t, nb, stride=8), :] = val[:, k * LANES:(k + 1) * LANES]


def _load_rows(ref, rows):
    return jnp.concatenate([ref[k, rows, :] for k in range(ref.shape[0])], axis=1)


def _sa_kernel(x_ref, sconv_ref, spool_ref, nw_ref, w_ref, cw_ref, cb_ref, dtb_ref, alog_ref,
               dsk_ref, pw_ref, ps_ref, e3_ref,
               z_ref, yp_ref, ypart_ref, ea_ref, xw_ref, cd3_ref, c_ref, b_ref, conv_ref, pool_ref,
               *, nb, nt):
    u = _rms(x_ref[...].reshape(nt * nb, -1), nw_ref[...]).astype(BF16)
    z_ref[...] = _dot(u, w_ref[:, O_Z:O_XBC]).reshape(nt, nb, D_SSD)
    xbc = _dot(u, w_ref[:, O_XBC:O_XP])
    xp = _dot(u, w_ref[:, O_XP + N_HEADS:O_XP + N_HEADS + POOL_W])
    dtr = _dot(u, w_ref[:, O_XP:O_XP + LANES])
    e3 = e3_ref[...]

    def tile(v, t):
        return v[t * nb:(t + 1) * nb, :]

    cext = [sconv_ref[k] for k in range(CONV_W - 1)]
    cext += [tile(xbc, t) for t in range(nt)]
    cw = cw_ref[...]
    xs, bm, cm = [], [], []
    for t in range(nt):
        acc = cb_ref[...]
        for k in range(CONV_W):
            acc = acc + cw[k:k + 1, :] * cext[t + k]
        v = _silu(acc)
        xs.append(v[:, 0:D_SSD])
        bm.append(v[:, D_SSD:D_SSD + N_GROUPS * D_STATE])
        cm.append(v[:, D_SSD + N_GROUPS * D_STATE:])
    for k in range(CONV_W - 1):
        conv_ref[:, k, :] = cext[nt + k]

    dt = _softplus(dtr + dtb_ref[...])
    a_row = -jnp.exp(alog_ref[...])
    dts = [tile(dt, t) for t in range(nt)]
    acum = []
    for t in range(nt):
        da = dts[t] * a_row
        acum.append(da if t == 0 else acum[-1] + da)
    a_last = acum[-1]

    for r in (xw_ref, cd3_ref, c_ref, b_ref):
        r[...] = jnp.zeros(r.shape, F32)
    lane = lax.broadcasted_iota(jnp.int32, (nb, LANES), 1)
    g0_heads = lane < HEADS_PER_GROUP
    dsk = dsk_ref[...]
    for t in range(nt):
        y = dsk * xs[t]
        for s in range(t + 1):
            sc = []
            for g in range(N_GROUPS):
                cg = cm[t][:, g * D_STATE:(g + 1) * D_STATE]
                bg = bm[s][:, g * D_STATE:(g + 1) * D_STATE]
                sc.append(jnp.sum(cg * bg, axis=1, keepdims=True))
            wts = jnp.where(g0_heads, sc[0], sc[1]) * jnp.exp(acum[t] - acum[s]) * dts[s]
            y = y + _expand_heads(wts, e3) * xs[s]
        ypart_ref[t] = y
        ea_ref[t] = _expand_heads(jnp.exp(acum[t]), e3)
        _store_seq8(xw_ref, t, _expand_heads(dts[t] * jnp.exp(a_last - acum[t]), e3) * xs[t])
        _store_seq8(c_ref, t, cm[t])
        _store_seq8(b_ref, t, bm[t])
    cd = jnp.exp(_expand_heads(a_last, e3))
    hi = cd.astype(BF16).astype(F32)
    mid = (cd - hi).astype(BF16).astype(F32)
    lo = (cd - hi - mid).astype(BF16).astype(F32)
    _store_seq8(cd3_ref, 0, hi)
    _store_seq8(cd3_ref, 1, mid)
    _store_seq8(cd3_ref, 2, lo)

    pext = [spool_ref[k] for k in range(POOL_BUF)]
    pext += [tile(xp, t) for t in range(nt)]
    for t in range(nt):
        yps = []
        for gi, w in enumerate(POOL_WINDOWS):
            cols = slice(gi * POOL_GD, (gi + 1) * POOL_GD)
            s = pext[POOL_BUF + t][:, cols]
            for k in range(1, w):
                s = s + pext[POOL_BUF + t - k][:, cols]
            d = s / float(w) - pext[POOL_BUF + t][:, cols]
            yps.append(_dot(d.astype(BF16), pw_ref[gi]))
        yp_ref[t] = (jnp.concatenate(yps, axis=1) * ps_ref[...]).astype(BF16)
    for k in range(POOL_BUF):
        pool_ref[:, k, :] = pext[nt + k]


def _sa_call(x_dec, sconv, spool, layer, w, depth, sb, conv_prev, pool_prev):
    groups, nt, nb, d = x_dec.shape
    in_specs = [
        pl.BlockSpec((None, nt, sb, d), lambda i: (groups - 1, 0, i, 0)),
        pl.BlockSpec((None, CONV_W - 1, sb, CONV_DIM), lambda i: (layer, 0, i, 0)),
        pl.BlockSpec((None, POOL_BUF, sb, POOL_W), lambda i: (layer, 0, i, 0)),
        _resident((1, d), layer),
        _resident((d, D_IN_PROJ), layer),
        _resident((CONV_W, CONV_DIM), layer),
        _resident((1, CONV_DIM), layer),
        _resident((1, LANES), layer),
        _resident((1, LANES), layer),
        _resident((1, D_SSD), layer),
        _resident((len(POOL_WINDOWS), POOL_GD, POOL_GD), layer),
        _resident((1, POOL_W), layer),
        _resident((3 * LANES, D_SSD)),
    ]
    args = [x_dec, sconv, spool, w["norm_mix"], w["w_in"], w["conv_w"], w["conv_b"],
            w["dt_bias"], w["a_log"], w["d_skip"], w["pool_w"], w["pool_scale"], w["e3"]]
    n_in = len(args)
    aliases = {}
    if conv_prev is not None:
        in_specs += [pl.BlockSpec(memory_space=pl.ANY), pl.BlockSpec(memory_space=pl.ANY)]
        aliases = {n_in: 8, n_in + 1: 9}
        args += [conv_prev, pool_prev]
    gn = N_GROUPS * D_STATE
    tm_outs = [(D_SSD, F32), (POOL_W, BF16), (D_SSD, F32), (D_SSD, F32)]
    out_specs = [pl.BlockSpec((nt, sb, wd), lambda i: (0, i, 0)) for wd, _ in tm_outs]
    out_shape = [jax.ShapeDtypeStruct((nt, nb, wd), dt) for wd, dt in tm_outs]
    for wd in (D_SSD, D_SSD, gn, gn):
        out_specs.append(pl.BlockSpec((wd // LANES, sb * 8, LANES), lambda i: (0, i, 0)))
        out_shape.append(jax.ShapeDtypeStruct((wd // LANES, nb * 8, LANES), F32))
    for n_rows, wd in ((CONV_W - 1, CONV_DIM), (POOL_BUF, POOL_W)):
        out_specs.append(pl.BlockSpec((None, sb, n_rows, wd), lambda i: (layer, i, 0, 0)))
        out_shape.append(jax.ShapeDtypeStruct((depth, nb, n_rows, wd), F32))

    def body(*refs):
        if conv_prev is not None:
            refs = refs[:n_in] + refs[n_in + 2:]
        _sa_kernel(*refs, nb=sb, nt=nt)

    return pl.pallas_call(
        body,
        grid=(nb // sb,),
        in_specs=in_specs,
        out_specs=out_specs,
        out_shape=out_shape,
        input_output_aliases=aliases,
        compiler_params=pltpu.CompilerParams(
            dimension_semantics=("arbitrary",), vmem_limit_bytes=VMEM_LIMIT),
        name="mixer_decode_tokens",
    )(*args)


def _sb_kernel(h_ref, c_ref, b_ref, xw_ref, cd3_ref, hn_ref, ch_ref, *, sblk):
    half = D_SSD // N_GROUPS
    ones = jnp.ones((8, D_STATE), BF16)
    for j in range(sblk):
        rows = slice(j * 8, (j + 1) * 8)
        h0 = h_ref[j]
        hb = h0.astype(BF16)
        cj = _load_rows(c_ref, rows).astype(BF16)
        bj = _load_rows(b_ref, rows).astype(BF16)
        xw = _load_rows(xw_ref, rows).astype(BF16)
        dmat = _dot_t0(_load_rows(cd3_ref, rows).astype(BF16), ones)
        chs, upds = [], []
        for g in range(N_GROUPS):
            rs = slice(g * half, (g + 1) * half)
            ns = slice(g * D_STATE, (g + 1) * D_STATE)
            chs.append(_dot_t1(cj[:, ns], hb[rs, :]))
            upds.append(_dot_t0(xw[:, rs], bj[:, ns]))
        ch = jnp.concatenate(chs, axis=1)
        for k in range(ch_ref.shape[0]):
            ch_ref[k, rows, :] = ch[:, k * LANES:(k + 1) * LANES]
        hn_ref[j] = h0 * dmat + jnp.concatenate(upds, axis=0)


def _sb_call(h0, c8, b8, xw8, cd38, layer, depth, nb, sblk, ssm_prev):
    gn = N_GROUPS * D_STATE
    in_specs = [
        pl.BlockSpec((None, sblk, D_SSD, D_STATE), lambda i: (layer, i, 0, 0)),
        pl.BlockSpec((gn // LANES, sblk * 8, LANES), lambda i: (0, i, 0)),
        pl.BlockSpec((gn // LANES, sblk * 8, LANES), lambda i: (0, i, 0)),
        pl.BlockSpec((D_SSD // LANES, sblk * 8, LANES), lambda i: (0, i, 0)),
        pl.BlockSpec((D_SSD // LANES, sblk * 8, LANES), lambda i: (0, i, 0)),
    ]
    args = [h0, c8, b8, xw8, cd38]
    n_in = len(args)
    aliases = {}
    if ssm_prev is not None:
        in_specs.append(pl.BlockSpec(memory_space=pl.ANY))
        aliases = {n_in: 0}
        args.append(ssm_prev)

    def body(*refs):
        if ssm_prev is not None:
            refs = refs[:n_in] + refs[n_in + 1:]
        _sb_kernel(*refs, sblk=sblk)

    return pl.pallas_call(
        body,
        grid=(nb // sblk,),
        in_specs=in_specs,
        out_specs=[
            pl.BlockSpec((None, sblk, D_SSD, D_STATE), lambda i: (layer, i, 0, 0)),
            pl.BlockSpec((D_SSD // LANES, sblk * 8, LANES), lambda i: (0, i, 0)),
        ],
        out_shape=[
            jax.ShapeDtypeStruct((depth, nb, D_SSD, D_STATE), F32),
            jax.ShapeDtypeStruct((D_SSD // LANES, nb * 8, LANES), F32),
        ],
        input_output_aliases=aliases,
        compiler_params=pltpu.CompilerParams(
            dimension_semantics=("parallel",), vmem_limit_bytes=VMEM_LIMIT),
        name="mixer_decode_state",
    )(*args)


def _sc_kernel(x_ref, z_ref, ypart_ref, ea_ref, ch_ref, yp_ref, snw_ref, wout_ref, o_ref, *, nb, nt):
    ch =jnp.concatenate([_load_rows(ch_ref, pl.ds(t, nb, stride=8)) for t in range(nt)], axis=0)
    y = ypart_ref[...] + ea_ref[...] * ch
    yn = _rms(y * _silu(z_ref[...]), snw_ref[...]).astype(BF16)
    out = _dot(jnp.concatenate([yn, yp_ref[...]], axis=1), wout_ref[...])
    o_ref[...] = x_ref[...] + out


def _sc_call(x, z, ypart, ea, ch8, yp, layer, w, nb, nt):
    t_all, d = x.shape
    rows = nt * nb
    xblk = (t_all - rows) // rows

    def full(shape):
        return pl.BlockSpec(shape, lambda i: (0, 0))

    return pl.pallas_call(
        functools.partial(_sc_kernel, nb=nb, nt=nt),
        grid=(1,),
        in_specs=[pl.BlockSpec((rows, d), lambda i: (xblk, 0)), full((rows, D_SSD)), full((rows, D_SSD)),
                  full((rows, D_SSD)), pl.BlockSpec((D_SSD // LANES, nb * 8, LANES), lambda i: (0, 0, 0)),
                  full((rows, POOL_W)),
                  _resident((1, D_SSD), layer), _resident((D_SSD + POOL_W, d), layer)],
        out_specs=pl.BlockSpec((rows, d), lambda i: (0, 0)),
        out_shape=jax.ShapeDtypeStruct((rows, d), F32),
        compiler_params=pltpu.CompilerParams(vmem_limit_bytes=VMEM_LIMIT),
        name="mixer_decode_out",
    )(x, z, ypart, ea, ch8, yp, w["ssd_norm_w"], w["w_out"])


def _prep_weights(w_in, conv_w, conv_b, dt_bias, a_log, d_skip, ssd_norm_w, pool_w, pool_scale, w_out,
                  norm_mix):
    depth = w_in.shape[0]
    pad_h = ((0, 0), (0, LANES - N_HEADS))
    head_of_lane = jnp.arange(D_SSD) // HEAD_DIM
    e1 = (jnp.arange(LANES)[:, None] == head_of_lane[None, :]).astype(BF16)
    return {
        "norm_mix": norm_mix.reshape(depth, 1, -1),
        "w_in": w_in.astype(BF16),
        "conv_w": conv_w,
        "conv_b": conv_b.reshape(depth, 1, -1),
        "dt_bias": jnp.pad(dt_bias, pad_h).reshape(depth, 1, LANES),
        "a_log": jnp.pad(a_log, pad_h).reshape(depth, 1, LANES),
        "d_skip": jnp.repeat(d_skip, HEAD_DIM, axis=-1).reshape(depth, 1, D_SSD),
        "ssd_norm_w": ssd_norm_w.reshape(depth, 1, -1),
        "pool_w": pool_w.astype(BF16),
        "pool_scale": pool_scale.reshape(depth, 1, -1),
        "w_out": w_out.astype(BF16),
        "e3": jnp.concatenate([e1, e1, e1], axis=0),
    }


def kernel(x_prompt, x_sample, p_prompt, p_sample, state_ssm, state_conv, state_pool, w_in, conv_w, conv_b,
           dt_bias, a_log, d_skip, ssd_norm_w, pool_w, pool_scale, w_out, norm_ffn1, ffn1_gate, ffn1_up,
           ffn1_down, norm_mix, norm_ffn2, ffn2_gate, ffn2_up, ffn2_down, norm_ple, ple_gate, ple_proj,
           final_norm):
    L, sblk = MIX_TILE, DEC_SEQ_BLOCK
    depth = w_in.shape[0]
    batch, seq, d = x_prompt.shape
    nb, nt, _ = x_sample.shape
    n_prompt, n_dec = batch * seq, nt * nb
    tm = n_dec
    n_main = n_prompt // tm
    assert n_main * tm == n_prompt
    mw = _prep_weights(w_in, conv_w, conv_b, dt_bias, a_log, d_skip, ssd_norm_w, pool_w, pool_scale, w_out,
                       norm_mix)
    f1 = (norm_ffn1.reshape(depth, 1, d), ffn1_gate, ffn1_up, ffn1_down)
    f2 = (norm_ffn2.reshape(depth, 1, d), ffn2_gate, ffn2_up, ffn2_down)
    ple = (p_prompt.reshape(depth, n_prompt, -1), jnp.transpose(p_sample, (0, 2, 1, 3)).reshape(depth, n_dec, -1),
           norm_ple.reshape(depth, 1, d), ple_gate.astype(BF16), ple_proj.astype(BF16), final_norm.reshape(1, d))
    x = (x_prompt.reshape(n_prompt, d), jnp.transpose(x_sample, (1, 0, 2)).reshape(n_dec, d))
    h_all = state_ssm.reshape(depth, nb, D_SSD, D_STATE)
    sconv_tm = jnp.transpose(state_conv, (0, 2, 1, 3))
    spool_tm = jnp.transpose(state_pool, (0, 2, 1, 3))

    ssm_p = ssm_s = conv_s = pool_s = None
    conv_p, pool_p = [], []
    for i in range(depth):
        last = i == depth - 1
        x = _ffn_call(x, i, *f1, tm, n_main)
        xm, ssm_p, cp, plp = _mixp_call(x, i, mw, depth, batch, seq, L, ssm_p)
        z, yp, ypart, ea, xw8, cd38, c8, b8, conv_s, pool_s = _sa_call(
            x.reshape(n_main + 1, nt, nb, d), sconv_tm, spool_tm, i, mw, depth, DEC_TOKEN_SEQ_BLOCK, conv_s, pool_s)
        z, yp, ypart, ea = (v.reshape(n_dec, -1) for v in (z, yp, ypart, ea))
        ssm_s, ch8 = _sb_call(h_all, c8, b8, xw8, cd38, i, depth, nb, sblk, ssm_s)
        xd = _sc_call(x, z, ypart, ea, ch8, yp, i, mw, nb, nt)
        x = _ffn_call((xm, xd), i, *f2, tm, n_main, ple=ple, final=last, split_out=last)
        conv_p.append(cp)
        pool_p.append(plp)
    y_prompt = x[0].reshape(batch, seq, d)
    y_sample = jnp.transpose(x[1].reshape(nt, nb, d), (1, 0, 2))
    return (y_prompt, y_sample,
            ssm_p.reshape(depth, batch, N_HEADS, HEAD_DIM, D_STATE), jnp.stack(conv_p), jnp.stack(pool_p),
            ssm_s.reshape(depth, nb, N_HEADS, HEAD_DIM, D_STATE),
            conv_s, pool_s)
```
